```python
import math
import jax, jax.numpy as jnp
from jax import lax
import numpy as np

D_MODEL = 2048
BATCH = 4
SEQ = 4096
DEPTH = 2
DEC_BATCH = 16
DEC_SEQ = 2048
PAST_LEN = 128

MLA_HEADS = 8
MLA_NOPE = 128
MLA_ROPE = 64
MLA_QK = MLA_NOPE + MLA_ROPE
MLA_V = 128
Q_LORA = 768
KV_LORA = 512
DIFF_HEADS = 8
DIFF_QK = 64
DIFF_V = 128
N_BRANCH = 2
N_EXPERTS = 16
EC_CAPACITY_FACTOR = 2
D_EXPERT = 4096
ROPE_THETA = 10000.0
EPS = 1e-6
Q_BLOCK = 128

IN_SIZES = (Q_LORA, KV_LORA, MLA_ROPE, DIFF_HEADS * 2 * DIFF_QK, DIFF_HEADS * 2 * DIFF_QK, DIFF_HEADS * DIFF_V, N_BRANCH * D_MODEL)
IN_OFFSETS = tuple(sum(IN_SIZES[: i + 1]) for i in range(len(IN_SIZES) - 1))
D_IN = sum(IN_SIZES)

kernel_name = "hybrid_mla_diffattn_ec_moe_encoder"


def _rmsnorm(x, g):
    xf = x.astype(jnp.float32)
    y = xf * lax.rsqrt(jnp.mean(xf * xf, axis=-1, keepdims=True) + EPS)
    return (y * g.astype(jnp.float32)).astype(x.dtype)


def _rope_tables(seq_len, dim):
    inv = 1.0 / (ROPE_THETA ** (jnp.arange(0, dim, 2, dtype=jnp.float32) / dim))
    ang = jnp.arange(seq_len, dtype=jnp.float32)[:, None] * inv[None, :]
    return jnp.cos(ang), jnp.sin(ang)


def _rope(x, cos, sin):
    half = x.shape[-1] // 2
    shape = (x.shape[1],) + (1,) * (x.ndim - 3) + (half,)
    c = cos.reshape(shape)
    s = sin.reshape(shape)
    xf = x.astype(jnp.float32)
    x1, x2 = xf[..., :half], xf[..., half:]
    return jnp.concatenate([x1 * c - x2 * s, x1 * s + x2 * c], axis=-1).astype(x.dtype)


def _mla_attention(q, k, v):
    b, s, h, dq = q.shape
    nb = s // Q_BLOCK
    scale = 1.0 / math.sqrt(dq)
    qb = q.reshape(b, nb, Q_BLOCK, h, dq).transpose(1, 0, 2, 3, 4)

    def block(qblk):
        sc = jnp.einsum("bqhd,bkhd->bhqk", qblk, k).astype(jnp.float32) * scale
        p = jax.nn.softmax(sc, axis=-1).astype(v.dtype)
        return jnp.einsum("bhqk,bkhd->bqhd", p, v)

    o = lax.map(block, qb)
    return o.transpose(1, 0, 2, 3, 4).reshape(b, s, h, v.shape[-1])


def _diff_attention(q1, q2, k1, k2, v, lam):
    b, s, h, dq = q1.shape
    nb = s // Q_BLOCK
    scale = 1.0 / math.sqrt(dq)
    q1b = q1.reshape(b, nb, Q_BLOCK, h, dq).transpose(1, 0, 2, 3, 4)
    q2b = q2.reshape(b, nb, Q_BLOCK, h, dq).transpose(1, 0, 2, 3, 4)

    def block(qs):
        qa, qc = qs
        s1 = jnp.einsum("bqhd,bkhd->bhqk", qa, k1).astype(jnp.float32) * scale
        s2 = jnp.einsum("bqhd,bkhd->bhqk", qc, k2).astype(jnp.float32) * scale
        p = jax.nn.softmax(s1, axis=-1) - lam * jax.nn.softmax(s2, axis=-1)
        return jnp.einsum("bhqk,bkhd->bqhd", p.astype(v.dtype), v)

    o = lax.map(block, (q1b, q2b))
    return o.transpose(1, 0, 2, 3, 4).reshape(b, s, h, v.shape[-1])


def _ec_moe(h, w_router, w_gate, w_up, w_down):
    n, d = h.shape
    cap = EC_CAPACITY_FACTOR * n // N_EXPERTS
    aff = jax.nn.softmax((h @ w_router).astype(jnp.float32), axis=-1)
    gate, idx = lax.top_k(aff.T, cap)
    xs = h[idx]
    hid = jax.nn.silu(jnp.einsum("ecd,edf->ecf", xs, w_gate)) * jnp.einsum("ecd,edf->ecf", xs, w_up)
    ye = jnp.einsum("ecf,efd->ecd", hid, w_down) * gate[..., None].astype(h.dtype)
    return jnp.zeros_like(h).at[idx.reshape(-1)].add(ye.reshape(-1, d))


def _layer(x, l, rope_a, rope_b, params):
    (norm_attn_g, w_in, b_gate, mla_cq_g, w_q_up, mla_ckv_g, w_kv_up, mla_q_g, mla_k_g, w_a_out,
     diff_q_g, diff_k_g, lambda_q1, lambda_k1, lambda_q2, lambda_k2, diff_sub_g, w_b_out, w_o,
     norm_ffn_g, w_router, w_e_gate, w_e_up, w_e_down) = params
    b, s, d = x.shape
    h = _rmsnorm(x, norm_attn_g[l])
    proj = h @ w_in[l]
    c_q, c_kv, k_pe, q_d, k_d, v_d, gates = jnp.split(proj, IN_OFFSETS, axis=-1)

    q_a = (_rmsnorm(c_q, mla_cq_g[l]) @ w_q_up[l]).reshape(b, s, MLA_HEADS, MLA_QK)
    kv = (_rmsnorm(c_kv, mla_ckv_g[l]) @ w_kv_up[l]).reshape(b, s, MLA_HEADS, MLA_NOPE + MLA_V)
    k_a = jnp.concatenate(
        [kv[..., :MLA_NOPE], jnp.broadcast_to(k_pe[:, :, None, :], (b, s, MLA_HEADS, MLA_ROPE))], axis=-1)
    v_a = kv[..., MLA_NOPE:]
    q_a = _rmsnorm(q_a, mla_q_g[l])
    k_a = _rmsnorm(k_a, mla_k_g[l])
    q_a = jnp.concatenate([q_a[..., :MLA_NOPE], _rope(q_a[..., MLA_NOPE:], rope_a[0], rope_a[1])], axis=-1)
    k_a = jnp.concatenate([k_a[..., :MLA_NOPE], _rope(k_a[..., MLA_NOPE:], rope_a[0], rope_a[1])], axis=-1)
    o_a = _mla_attention(q_a, k_a, v_a).reshape(b, s, MLA_HEADS * MLA_V) @ w_a_out[l]

    q_d = _rope(_rmsnorm(q_d.reshape(b, s, DIFF_HEADS, 2, DIFF_QK), diff_q_g[l]), rope_b[0], rope_b[1])
    k_d = _rope(_rmsnorm(k_d.reshape(b, s, DIFF_HEADS, 2, DIFF_QK), diff_k_g[l]), rope_b[0], rope_b[1])
    v_d = v_d.reshape(b, s, DIFF_HEADS, DIFF_V)
    lam_init = 0.8 - 0.6 * math.exp(-0.3 * l)
    lam = (jnp.exp(jnp.sum(lambda_q1[l].astype(jnp.float32) * lambda_k1[l].astype(jnp.float32)))
           - jnp.exp(jnp.sum(lambda_q2[l].astype(jnp.float32) * lambda_k2[l].astype(jnp.float32)))
           + lam_init)
    o_d = _diff_attention(q_d[..., 0, :], q_d[..., 1, :], k_d[..., 0, :], k_d[..., 1, :], v_d, lam)
    o_d = _rmsnorm(o_d, diff_sub_g[l]) * (1.0 - lam_init)
    o_b = o_d.reshape(b, s, DIFF_HEADS * DIFF_V) @ w_b_out[l]

    g = jax.nn.sigmoid(gates.astype(jnp.float32) + b_gate[l].astype(jnp.float32))
    g = g.reshape(b, s, N_BRANCH, d).astype(x.dtype)
    x = x + (g[:, :, 0] * o_a + g[:, :, 1] * o_b) @ w_o[l]

    h2 = _rmsnorm(x, norm_ffn_g[l]).reshape(b * s, d)
    x = x + _ec_moe(h2, w_router[l], w_e_gate[l], w_e_up[l], w_e_down[l]).reshape(b, s, d)
    return x


def _trunk(x, params):
    s = x.shape[1]
    rope_a = _rope_tables(s, MLA_ROPE)
    rope_b = _rope_tables(s, DIFF_QK)
    for l in range(DEPTH):
        x = _layer(x, l, rope_a, rope_b, params)
    return x


def setup_inputs(seed: int = 0) -> dict:
    key = jax.random.key(seed)
    ks = jax.random.split(key, 32)

    def nrm(k, shape, scale):
        return jax.random.normal(k, shape, dtype=jnp.float32) * scale

    def gain(k, shape):
        return 1.0 + 0.01 * jax.random.normal(k, shape, dtype=jnp.float32)

    D = D_MODEL
    return {
        "x_prompt": nrm(ks[0], (BATCH, SEQ, D), 1.0),
        "x_sample": nrm(ks[1], (DEC_BATCH, DEC_SEQ, D), 1.0),
        "norm_attn_g": gain(ks[2], (DEPTH, D)),
        "w_in": nrm(ks[3], (DEPTH, D, D_IN), D ** -0.5),
        "b_gate": nrm(ks[4], (DEPTH, N_BRANCH * D), 0.01),
        "mla_cq_g": gain(ks[5], (DEPTH, Q_LORA)),
        "w_q_up": nrm(ks[6], (DEPTH, Q_LORA, MLA_HEADS * MLA_QK), Q_LORA ** -0.5),
        "mla_ckv_g": gain(ks[7], (DEPTH, KV_LORA)),
        "w_kv_up": nrm(ks[8], (DEPTH, KV_LORA, MLA_HEADS * (MLA_NOPE + MLA_V)), KV_LORA ** -0.5),
        "mla_q_g": gain(ks[9], (DEPTH, MLA_QK)),
        "mla_k_g": gain(ks[10], (DEPTH, MLA_QK)),
        "w_a_out": nrm(ks[11], (DEPTH, MLA_HEADS * MLA_V, D), (MLA_HEADS * MLA_V) ** -0.5),
        "diff_q_g": gain(ks[12], (DEPTH, DIFF_QK)),
        "diff_k_g": gain(ks[13], (DEPTH, DIFF_QK)),
        "lambda_q1": nrm(ks[14], (DEPTH, DIFF_QK), 0.1),
        "lambda_k1": nrm(ks[15], (DEPTH, DIFF_QK), 0.1),
        "lambda_q2": nrm(ks[16], (DEPTH, DIFF_QK), 0.1),
        "lambda_k2": nrm(ks[17], (DEPTH, DIFF_QK), 0.1),
        "diff_sub_g": gain(ks[18], (DEPTH, DIFF_V)),
        "w_b_out": nrm(ks[19], (DEPTH, DIFF_HEADS * DIFF_V, D), (DIFF_HEADS * DIFF_V) ** -0.5),
        "w_o": nrm(ks[20], (DEPTH, D, D), D ** -0.5),
        "norm_ffn_g": gain(ks[21], (DEPTH, D)),
        "w_router": nrm(ks[22], (DEPTH, D, N_EXPERTS), D ** -0.5),
        "w_e_gate": nrm(ks[23], (DEPTH, N_EXPERTS, D, D_EXPERT), D ** -0.5),
        "w_e_up": nrm(ks[24], (DEPTH, N_EXPERTS, D, D_EXPERT), D ** -0.5),
        "w_e_down": nrm(ks[25], (DEPTH, N_EXPERTS, D_EXPERT, D), D_EXPERT ** -0.5),
    }


def reference(x_prompt, x_sample, norm_attn_g, w_in, b_gate, mla_cq_g, w_q_up, mla_ckv_g, w_kv_up,
              mla_q_g, mla_k_g, w_a_out, diff_q_g, diff_k_g, lambda_q1, lambda_k1, lambda_q2, lambda_k2,
              diff_sub_g, w_b_out, w_o, norm_ffn_g, w_router, w_e_gate, w_e_up, w_e_down):
    params = (norm_attn_g, w_in, b_gate, mla_cq_g, w_q_up, mla_ckv_g, w_kv_up, mla_q_g, mla_k_g, w_a_out,
              diff_q_g, diff_k_g, lambda_q1, lambda_k1, lambda_q2, lambda_k2, diff_sub_g, w_b_out, w_o,
              norm_ffn_g, w_router, w_e_gate, w_e_up, w_e_down)
    y_prompt = _trunk(x_prompt, params)
    y_sample = _trunk(x_sample, params)
    return (y_prompt, y_sample)
```

```python
import functools
import math

import jax
import jax.numpy as jnp
from jax import lax
from jax.experimental import pallas as pl
from jax.experimental.pallas import tpu as pltpu

F32 = jnp.float32
BF16 = jnp.bfloat16

MLA_HEADS = 8
MLA_NOPE = 128
MLA_ROPE = 64
MLA_QK = MLA_NOPE + MLA_ROPE
MLA_V = 128
Q_LORA = 768
KV_LORA = 512
DIFF_HEADS = 8
DIFF_QK = 64
DIFF_V = 128
N_EXPERTS = 16
EC_CAPACITY_FACTOR = 2
ROPE_THETA = 10000.0
EPS = 1e-6

LANES = 128
HEAD_PAD = 2 * LANES
MLA_SEG = 1536
VMEM_LIMIT = 56 * 1024 * 1024


def _cparams(*sem):
    return pltpu.CompilerParams(dimension_semantics=sem, vmem_limit_bytes=VMEM_LIMIT)


def _rms(x, g):
    ms = jnp.mean(x * x, axis=-1, keepdims=True)
    return x * lax.rsqrt(ms + EPS) * g


def _dot(a, b):
    return jnp.dot(a, b, preferred_element_type=F32)


def _dot_nt(a, b, precision=None):
    return lax.dot_general(a, b, (((1,), (1,)), ((), ())), preferred_element_type=F32, precision=precision)


def _sigmoid(z):
    return 1.0 / (1.0 + jnp.exp(-z))


def _tile(n, t):
    t = min(n, t)
    assert n % t == 0, (n, t)
    return t


def _norm_mm_kernel(x_ref, g_ref, w_ref, o_ref, xn_ref):
    @pl.when(pl.program_id(1) == 0)
    def _():
        xn_ref[...] = _rms(x_ref[...].astype(F32), g_ref[...]).astype(BF16)

    o_ref[...] = _dot(xn_ref[...], w_ref[...]).astype(o_ref.dtype)


def _norm_mm(x, g, w, out_dtype, tm=1024, tn=512):
    m, k = x.shape
    n = w.shape[1]
    tm, tn = _tile(m, tm), _tile(n, tn)
    return pl.pallas_call(
        _norm_mm_kernel,
        grid=(m // tm, n // tn),
        in_specs=[
            pl.BlockSpec((tm, k), lambda i, j: (i, 0)),
            pl.BlockSpec((1, k), lambda i, j: (0, 0)),
            pl.BlockSpec((k, tn), lambda i, j: (0, j)),
        ],
        out_specs=pl.BlockSpec((tm, tn), lambda i, j: (i, j)),
        out_shape=jax.ShapeDtypeStruct((m, n), out_dtype),
        scratch_shapes=[pltpu.VMEM((tm, k), BF16)],
        compiler_params=_cparams("parallel", "arbitrary"),
        name="in_proj",
    )(x, g, w)


def _rope128(t, cos, s1, s2):
    return t * cos + pltpu.roll(t, 96, 1) * s1 + pltpu.roll(t, 32, 1) * s2


def _mla_prep_kernel(p_ref, cqg_ref, ckvg_ref, wq_ref, wkv_ref, qg_ref, kg_ref, cos_ref, s1_ref, s2_ref,
                     q_out, k_out, v_out):
    p = p_ref[...]
    cq = p[:, :Q_LORA].astype(F32)
    ckv = p[:, Q_LORA:Q_LORA + KV_LORA].astype(F32)
    kpe = p[:, Q_LORA + KV_LORA:Q_LORA + KV_LORA + LANES].astype(F32)
    q = _dot(_rms(cq, cqg_ref[...]).astype(BF16), wq_ref[...])
    kv = _dot(_rms(ckv, ckvg_ref[...]).astype(BF16), wkv_ref[...])
    cos, s1, s2 = cos_ref[...], s1_ref[...], s2_ref[...]
    qg, kg = qg_ref[...], kg_ref[...]
    scale = 1.0 / math.sqrt(MLA_QK)
    kpe_sq = jnp.sum(kpe * kpe, axis=-1, keepdims=True)
    kpe_r = _rope128(kpe * kg[:, LANES:], cos, s1, s2)
    nk = MLA_HEADS * MLA_NOPE
    for h in range(MLA_HEADS):
        qh = q[:, h * HEAD_PAD:(h + 1) * HEAD_PAD]
        r = lax.rsqrt(jnp.sum(qh * qh, axis=-1, keepdims=True) / MLA_QK + EPS) * scale
        q_out[:, h * HEAD_PAD:h * HEAD_PAD + LANES] = (qh[:, :LANES] * r * qg[:, :LANES]).astype(BF16)
        q_hi = _rope128(qh[:, LANES:] * qg[:, LANES:], cos, s1, s2) * r
        q_out[:, h * HEAD_PAD + LANES:(h + 1) * HEAD_PAD] = q_hi.astype(BF16)
        kn = kv[:, h * MLA_NOPE:(h + 1) * MLA_NOPE]
        rk = lax.rsqrt((jnp.sum(kn * kn, axis=-1, keepdims=True) + kpe_sq) / MLA_QK + EPS)
        k_out[:, h * HEAD_PAD:h * HEAD_PAD + LANES] = (kn * rk * kg[:, :LANES]).astype(BF16)
        k_out[:, h * HEAD_PAD + LANES:(h + 1) * HEAD_PAD] = (kpe_r * rk).astype(BF16)
    v_out[...] = kv[:, nk:].astype(BF16)


def _mla_prep(proj, cqg, ckvg, wq, wkv, qg, kg, rope, seq, tm=512):
    m = proj.shape[0]
    tm = _tile(seq, tm)
    nsb = seq // tm
    full = lambda a: pl.BlockSpec(a.shape, lambda i: (0,) * a.ndim)
    rspec = pl.BlockSpec((tm, LANES), lambda i: (i % nsb, 0))
    hq = MLA_HEADS * HEAD_PAD
    return pl.pallas_call(
        _mla_prep_kernel,
        grid=(m // tm,),
        in_specs=[pl.BlockSpec((tm, MLA_SEG), lambda i: (i, 0)), full(cqg), full(ckvg), full(wq), full(wkv),
                  full(qg), full(kg), rspec, rspec, rspec],
        out_specs=[pl.BlockSpec((tm, hq), lambda i: (i, 0)), pl.BlockSpec((tm, hq), lambda i: (i, 0)),
                   pl.BlockSpec((tm, MLA_HEADS * MLA_V), lambda i: (i, 0))],
        out_shape=[jax.ShapeDtypeStruct((m, hq), BF16), jax.ShapeDtypeStruct((m, hq), BF16),
                   jax.ShapeDtypeStruct((m, MLA_HEADS * MLA_V), BF16)],
        compiler_params=_cparams("parallel"),
        name="mla_prep",
    )(proj, cqg, ckvg, wq, wkv, qg, kg, *rope)


def _diff_prep_kernel(q_ref, k_ref, qg_ref, kg_ref, cos_ref, s1_ref, s2_ref, q_out, k_out):
    cos, s1, s2 = cos_ref[...], s1_ref[...], s2_ref[...]
    lane = lax.broadcasted_iota(jnp.int32, (1, LANES), 1)
    lo = lane < DIFF_QK
    nh = q_ref.shape[1] // LANES

    def prep(ref, g, out, scale):
        for h in range(nh):
            x = ref[:, h * LANES:(h + 1) * LANES].astype(F32)
            x2 = x * x
            s_lo = jnp.sum(jnp.where(lo, x2, 0.0), axis=-1, keepdims=True)
            s_hi = jnp.sum(jnp.where(lo, 0.0, x2), axis=-1, keepdims=True)
            r = lax.rsqrt(jnp.where(lo, s_lo, s_hi) / DIFF_QK + EPS)
            out[:, h * LANES:(h + 1) * LANES] = (_rope128(x * r * g, cos, s1, s2) * scale).astype(BF16)

    prep(q_ref, qg_ref[...], q_out, 1.0 / math.sqrt(DIFF_QK))
    prep(k_ref, kg_ref[...], k_out, 1.0)


def _diff_prep(proj, q_col, k_col, qg, kg, rope, seq, tm=512, tn=512):
    m = proj.shape[0]
    width = DIFF_HEADS * 2 * DIFF_QK
    tm = _tile(seq, tm)
    nsb = seq // tm
    nj = width // tn
    full = lambda a: pl.BlockSpec(a.shape, lambda i, j: (0,) * a.ndim)
    rspec = pl.BlockSpec((tm, LANES), lambda i, j: (i % nsb, 0))
    ospec = pl.BlockSpec((tm, tn), lambda i, j: (i, j))
    return pl.pallas_call(
        _diff_prep_kernel,
        grid=(m // tm, nj),
        in_specs=[pl.BlockSpec((tm, tn), lambda i, j: (i, q_col // tn + j)),
                  pl.BlockSpec((tm, tn), lambda i, j: (i, k_col // tn + j)),
                  full(qg), full(kg), rspec, rspec, rspec],
        out_specs=[ospec, ospec],
        out_shape=[jax.ShapeDtypeStruct((m, width), BF16)] * 2,
        compiler_params=_cparams("parallel", "parallel"),
        name="diff_prep",
    )(proj, proj, qg, kg, *rope)


def _mla_attn_kernel(q_ref, k_ref, v_ref, o_ref):
    s = _dot_nt(q_ref[...], k_ref[...])
    p = jnp.exp(s - jnp.max(s, axis=-1, keepdims=True))
    l = jnp.sum(p, axis=-1, keepdims=True)
    o = _dot(p.astype(BF16), v_ref[...])
    o_ref[...] = (o / l).astype(o_ref.dtype)


def _mla_attn(q, k, v, batch, seq, tq=256):
    m = q.shape[0]
    tq = _tile(seq, tq)
    nq = seq // tq
    return pl.pallas_call(
        _mla_attn_kernel,
        grid=(batch, MLA_HEADS, nq),
        in_specs=[pl.BlockSpec((tq, HEAD_PAD), lambda b, h, i: (b * nq + i, h)),
                  pl.BlockSpec((seq, HEAD_PAD), lambda b, h, i: (b, h)),
                  pl.BlockSpec((seq, MLA_V), lambda b, h, i: (b, h))],
        out_specs=pl.BlockSpec((tq, MLA_V), lambda b, h, i: (b * nq + i, h)),
        out_shape=jax.ShapeDtypeStruct((m, MLA_HEADS * MLA_V), BF16),
        compiler_params=_cparams("parallel", "parallel", "arbitrary"),
        name="mla_attn",
    )(q, k, v)


def _diff_attn_kernel(lam_ref, q_ref, k_ref, v_ref, g_ref, o_ref, *, out_scale):
    q, k = q_ref[...], k_ref[...]
    lane = lax.broadcasted_iota(jnp.int32, (1, LANES), 1)
    lo = lane < DIFF_QK
    zero = jnp.zeros_like(q)
    s1 = _dot_nt(jnp.where(lo, q, zero), k)
    s2 = _dot_nt(jnp.where(lo, zero, q), k)
    e1 = jnp.exp(s1 - jnp.max(s1, axis=-1, keepdims=True))
    e2 = jnp.exp(s2 - jnp.max(s2, axis=-1, keepdims=True))
    c1 = 1.0 / jnp.sum(e1, axis=-1, keepdims=True)
    c2 = lam_ref[0] / jnp.sum(e2, axis=-1, keepdims=True)
    p = e1 * c1 - e2 * c2
    o = _dot(p.astype(BF16), v_ref[...])
    o_ref[...] = (_rms(o, g_ref[...]) * out_scale).astype(o_ref.dtype)


def _diff_attn(lam, q, k, proj, v_col, sub_g, out_scale, batch, seq, tq=256):
    m = q.shape[0]
    tq = _tile(seq, tq)
    nq = seq // tq
    vb = v_col // DIFF_V
    return pl.pallas_call(
        functools.partial(_diff_attn_kernel, out_scale=out_scale),
        grid=(batch, DIFF_HEADS, nq),
        in_specs=[pl.BlockSpec(memory_space=pltpu.SMEM),
                  pl.BlockSpec((tq, LANES), lambda b, h, i: (b * nq + i, h)),
                  pl.BlockSpec((seq, LANES), lambda b, h, i: (b, h)),
                  pl.BlockSpec((seq, DIFF_V), lambda b, h, i: (b, vb + h)),
                  pl.BlockSpec((1, DIFF_V), lambda b, h, i: (0, 0))],
        out_specs=pl.BlockSpec((tq, DIFF_V), lambda b, h, i: (b * nq + i, h)),
        out_shape=jax.ShapeDtypeStruct((m, DIFF_HEADS * DIFF_V), BF16),
        compiler_params=_cparams("parallel", "parallel", "arbitrary"),
        name="diff_attn",
    )(lam, q, k, proj, sub_g)


def _merge_kernel(oa_ref, ob_ref, wa_ref, wb_ref, g0_ref, g1_ref, b0_ref, b1_ref, o_ref):
    a = _dot(oa_ref[...], wa_ref[...])
    b = _dot(ob_ref[...], wb_ref[...])
    g0 = _sigmoid(g0_ref[...].astype(F32) + b0_ref[...])
    g1 = _sigmoid(g1_ref[...].astype(F32) + b1_ref[...])
    o_ref[...] = (g0 * a + g1 * b).astype(o_ref.dtype)


def _merge(oa, ob, wa, wb, proj, gate_col, b_gate, tm=512, tn=512):
    m, ka = oa.shape
    kb = ob.shape[1]
    d = wa.shape[1]
    tm, tn = _tile(m, tm), _tile(d, tn)
    nj = d // tn
    g0b = gate_col // tn
    return pl.pallas_call(
        _merge_kernel,
        grid=(m // tm, nj),
        in_specs=[pl.BlockSpec((tm, ka), lambda i, j: (i, 0)),
                  pl.BlockSpec((tm, kb), lambda i, j: (i, 0)),
                  pl.BlockSpec((ka, tn), lambda i, j: (0, j)),
                  pl.BlockSpec((kb, tn), lambda i, j: (0, j)),
                  pl.BlockSpec((tm, tn), lambda i, j: (i, g0b + j)),
                  pl.BlockSpec((tm, tn), lambda i, j: (i, g0b + nj + j)),
                  pl.BlockSpec((1, tn), lambda i, j: (0, j)),
                  pl.BlockSpec((1, tn), lambda i, j: (0, nj + j))],
        out_specs=pl.BlockSpec((tm, tn), lambda i, j: (i, j)),
        out_shape=jax.ShapeDtypeStruct((m, d), BF16),
        compiler_params=_cparams("parallel", "arbitrary"),
        name="merge",
    )(oa, ob, wa, wb, proj, proj, b_gate, b_gate)


def _mm_res_kernel(a_ref, w_ref, r_ref, o_ref):
    o_ref[...] = r_ref[...] + _dot(a_ref[...], w_ref[...])


def _mm_res(a, w, res, tm=512, tn=512):
    m, k = a.shape
    n = w.shape[1]
    tm, tn = _tile(m, tm), _tile(n, tn)
    return pl.pallas_call(
        _mm_res_kernel,
        grid=(m // tm, n // tn),
        in_specs=[pl.BlockSpec((tm, k), lambda i, j: (i, 0)),
                  pl.BlockSpec((k, tn), lambda i, j: (0, j)),
                  pl.BlockSpec((tm, tn), lambda i, j: (i, j))],
        out_specs=pl.BlockSpec((tm, tn), lambda i, j: (i, j)),
        out_shape=jax.ShapeDtypeStruct((m, n), F32),
        compiler_params=_cparams("parallel", "arbitrary"),
        name="out_proj",
    )(a, w, res)


def _router_kernel(x_ref, g_ref, wr_ref, h_ref, aff_ref):
    xn = _rms(x_ref[...], g_ref[...])
    h_ref[...] = xn.astype(BF16)
    lg = _dot_nt(wr_ref[...], xn, precision=lax.Precision.HIGHEST)
    e = jnp.exp(lg - jnp.max(lg, axis=0, keepdims=True))
    aff_ref[...] = e / jnp.sum(e, axis=0, keepdims=True)


def _router(x, g, wr_t, tm=512):
    m, d = x.shape
    ne = wr_t.shape[0]
    tm = _tile(m, tm)
    return pl.pallas_call(
        _router_kernel,
        grid=(m // tm,),
        in_specs=[pl.BlockSpec((tm, d), lambda i: (i, 0)),
                  pl.BlockSpec((1, d), lambda i: (0, 0)),
                  pl.BlockSpec((ne, d), lambda i: (0, 0))],
        out_specs=[pl.BlockSpec((tm, d), lambda i: (i, 0)), pl.BlockSpec((ne, tm), lambda i: (0, i))],
        out_shape=[jax.ShapeDtypeStruct((m, d), BF16), jax.ShapeDtypeStruct((ne, m), F32)],
        compiler_params=_cparams("parallel"),
        name="router",
    )(x, g, wr_t)


def _ffn_kernel(x_ref, wg_ref, wu_ref, wd_ref, gate_ref, o_ref, acc_ref):
    f = pl.program_id(2)
    x = x_ref[...]
    hg = _dot(x, wg_ref[0])
    hu = _dot(x, wu_ref[0])
    hid = (hg * _sigmoid(hg) * hu).astype(BF16)
    part = _dot(hid, wd_ref[0])

    @pl.when(f == 0)
    def _():
        acc_ref[...] = part

    @pl.when(f > 0)
    def _():
        acc_ref[...] += part

    @pl.when(f == pl.num_programs(2) - 1)
    def _():
        o_ref[...] = (acc_ref[...] * gate_ref[...]).astype(o_ref.dtype)


def _ffn(xs, wg, wu, wd, gate, cap, t=1024, tf=512):
    ne, d, fdim = wg.shape
    t, tf = _tile(cap, t), _tile(fdim, tf)
    nt = cap // t
    return pl.pallas_call(
        _ffn_kernel,
        grid=(ne, nt, fdim // tf),
        in_specs=[pl.BlockSpec((t, d), lambda e, i, f: (e * nt + i, 0)),
                  pl.BlockSpec((1, d, tf), lambda e, i, f: (e, 0, f)),
                  pl.BlockSpec((1, d, tf), lambda e, i, f: (e, 0, f)),
                  pl.BlockSpec((1, tf, d), lambda e, i, f: (e, f, 0)),
                  pl.BlockSpec((t, 1), lambda e, i, f: (e * nt + i, 0))],
        out_specs=pl.BlockSpec((t, d), lambda e, i, f: (e * nt + i, 0)),
        out_shape=jax.ShapeDtypeStruct((ne * cap, d), F32),
        scratch_shapes=[pltpu.VMEM((t, d), F32)],
        compiler_params=_cparams("parallel", "parallel", "arbitrary"),
        name="expert_ffn",
    )(xs, wg, wu, wd, gate)


def _rope_tables(seq):
    def tables(dim, reps):
        inv = 1.0 / (ROPE_THETA ** (jnp.arange(0, dim, 2, dtype=F32) / dim))
        ang = jnp.arange(seq, dtype=F32)[:, None] * inv[None, :]
        c, s = jnp.cos(ang), jnp.sin(ang)
        z = jnp.zeros_like(c)
        pad = jnp.zeros((seq, LANES - reps * dim), F32)
        cos = jnp.concatenate([c, c] * reps + [pad], axis=-1)
        s1 = jnp.concatenate([-s, z] * reps + [pad], axis=-1)
        s2 = jnp.concatenate([z, s] * reps + [pad], axis=-1)
        return cos, s1, s2

    return tables(MLA_ROPE, 1), tables(DIFF_QK, LANES // DIFF_QK)


def _prep_layer(l, p):
    d = p["w_in"].shape[1]
    w_in = p["w_in"][l]
    o_cq, o_ckv, o_kpe = 0, Q_LORA, Q_LORA + KV_LORA
    o_qd = o_kpe + MLA_ROPE
    dw = DIFF_HEADS * 2 * DIFF_QK
    o_kd, o_vd = o_qd + dw, o_qd + 2 * dw
    o_g = o_vd + DIFF_HEADS * DIFF_V
    w_in_p = jnp.concatenate(
        [w_in[:, :o_qd], jnp.zeros((d, MLA_SEG - o_qd), F32), w_in[:, o_g:], w_in[:, o_qd:o_g]], axis=1).astype(BF16)
    cols = dict(gate=MLA_SEG, qd=MLA_SEG + 2 * d, kd=MLA_SEG + 2 * d + dw, vd=MLA_SEG + 2 * d + 2 * dw)
    wq = p["w_q_up"][l].reshape(Q_LORA, MLA_HEADS, MLA_QK)
    wq = jnp.pad(wq, ((0, 0), (0, 0), (0, HEAD_PAD - MLA_QK))).reshape(Q_LORA, MLA_HEADS * HEAD_PAD).astype(BF16)
    wkv = p["w_kv_up"][l].reshape(KV_LORA, MLA_HEADS, MLA_NOPE + MLA_V)
    wkv = jnp.concatenate([wkv[:, :, :MLA_NOPE].reshape(KV_LORA, -1), wkv[:, :, MLA_NOPE:].reshape(KV_LORA, -1)],
                          axis=1).astype(BF16)
    row = lambda a: a.reshape(1, -1).astype(F32)
    pad_g = lambda a: jnp.pad(a, (0, HEAD_PAD - MLA_QK)).reshape(1, -1)
    lam_init = 0.8 - 0.6 * math.exp(-0.3 * l)
    lam = (jnp.exp(jnp.sum(p["lambda_q1"][l].astype(F32) * p["lambda_k1"][l].astype(F32)))
           - jnp.exp(jnp.sum(p["lambda_q2"][l].astype(F32) * p["lambda_k2"][l].astype(F32))) + lam_init)
    return dict(
        cols=cols, w_in=w_in_p, norm_attn_g=row(p["norm_attn_g"][l]), b_gate=row(p["b_gate"][l]),
        cq_g=row(p["mla_cq_g"][l]), ckv_g=row(p["mla_ckv_g"][l]), wq=wq, wkv=wkv,
        q_g=pad_g(p["mla_q_g"][l]), k_g=pad_g(p["mla_k_g"][l]),
        dq_g=jnp.tile(p["diff_q_g"][l], LANES // DIFF_QK).reshape(1, -1),
        dk_g=jnp.tile(p["diff_k_g"][l], LANES // DIFF_QK).reshape(1, -1),
        lam=lam.reshape(1).astype(F32), out_scale=1.0 - lam_init, sub_g=row(p["diff_sub_g"][l]),
        wa=p["w_a_out"][l].astype(BF16), wb=p["w_b_out"][l].astype(BF16), wo=p["w_o"][l].astype(BF16),
        norm_ffn_g=row(p["norm_ffn_g"][l]), wr_t=p["w_router"][l].T.astype(F32),
        wg=p["w_e_gate"][l].astype(BF16), wu=p["w_e_up"][l].astype(BF16), wd=p["w_e_down"][l].astype(BF16),
    )


def _layer(x, lp, ropes, batch, seq):
    m, d = x.shape
    rope_a, rope_b = ropes
    cols = lp["cols"]
    proj = _norm_mm(x, lp["norm_attn_g"], lp["w_in"], BF16)
    q_a, k_a, v_a = _mla_prep(proj, lp["cq_g"], lp["ckv_g"], lp["wq"], lp["wkv"], lp["q_g"], lp["k_g"], rope_a, seq)
    o_a = _mla_attn(q_a, k_a, v_a, batch, seq)
    q_d, k_d = _diff_prep(proj, cols["qd"], cols["kd"], lp["dq_g"], lp["dk_g"], rope_b, seq)
    o_b = _diff_attn(lp["lam"], q_d, k_d, proj, cols["vd"], lp["sub_g"], lp["out_scale"], batch, seq)
    merged = _merge(o_a, o_b, lp["wa"], lp["wb"], proj, cols["gate"], lp["b_gate"])
    x = _mm_res(merged, lp["wo"], x)

    h2, aff_t = _router(x, lp["norm_ffn_g"], lp["wr_t"])
    cap = EC_CAPACITY_FACTOR * m // N_EXPERTS
    gate, idx = lax.top_k(aff_t, cap)
    xs = h2[idx.reshape(-1)]
    ye = _ffn(xs, lp["wg"], lp["wu"], lp["wd"], gate.reshape(-1, 1), cap)
    return x.at[idx.reshape(-1)].add(ye)


def kernel(x_prompt, x_sample, norm_attn_g, w_in, b_gate, mla_cq_g, w_q_up, mla_ckv_g, w_kv_up, mla_q_g, mla_k_g,
           w_a_out, diff_q_g, diff_k_g, lambda_q1, lambda_k1, lambda_q2, lambda_k2, diff_sub_g, w_b_out, w_o,
           norm_ffn_g, w_router, w_e_gate, w_e_up, w_e_down):
    p = dict(norm_attn_g=norm_attn_g, w_in=w_in, b_gate=b_gate, mla_cq_g=mla_cq_g, w_q_up=w_q_up,
             mla_ckv_g=mla_ckv_g, w_kv_up=w_kv_up, mla_q_g=mla_q_g, mla_k_g=mla_k_g, w_a_out=w_a_out,
             diff_q_g=diff_q_g, diff_k_g=diff_k_g, lambda_q1=lambda_q1, lambda_k1=lambda_k1, lambda_q2=lambda_q2,
             lambda_k2=lambda_k2, diff_sub_g=diff_sub_g, w_b_out=w_b_out, w_o=w_o, norm_ffn_g=norm_ffn_g,
             w_router=w_router, w_e_gate=w_e_gate, w_e_up=w_e_up, w_e_down=w_e_down)
    depth = w_in.shape[0]
    groups = [x_prompt, x_sample]
    shapes = [g.shape for g in groups]
    xs = [g.reshape(-1, g.shape[-1]) for g in groups]
    ropes = [_rope_tables(s[1]) for s in shapes]
    for l in range(depth):
        lp = _prep_layer(l, p)
        xs = [_layer(x, lp, r, s[0], s[1]) for x, r, s in zip(xs, ropes, shapes)]
    return tuple(x.reshape(s) for x, s in zip(xs, shapes))
```

```python
import functools
import math

import jax
import jax.numpy as jnp
from jax import lax
from jax.experimental import pallas as pl
from jax.experimental.pallas import tpu as pltpu

F32 = jnp.float32
BF16 = jnp.bfloat16

MLA_HEADS = 8
MLA_NOPE = 128
MLA_ROPE = 64
MLA_QK = MLA_NOPE + MLA_ROPE
MLA_V = 128
Q_LORA = 768
KV_LORA = 512
DIFF_HEADS = 8
DIFF_QK = 64
DIFF_V = 128
N_EXPERTS = 16
EC_CAPACITY_FACTOR = 2
ROPE_THETA = 10000.0
EPS = 1e-6

LOG2E = 1.4426950408889634
LANES = 128
HEAD_PAD = 2 * LANES
MLA_SEG = 1536
VMEM_LIMIT = 56 * 1024 * 1024
CHAIN_ROWS = 256
MLA_SCORE_ELEMS = 2048 * 4096
DIFF_SCORE_ELEMS = 2048 * 2048


def _cparams(*sem):
    return pltpu.CompilerParams(dimension_semantics=sem, vmem_limit_bytes=VMEM_LIMIT)


def _rms(x, g):
    ms = jnp.mean(x * x, axis=-1, keepdims=True)
    return x * lax.rsqrt(ms + EPS) * g


def _dot(a, b):
    return jnp.dot(a, b, preferred_element_type=F32)


def _dot_nt(a, b, precision=None):
    return lax.dot_general(a, b, (((1,), (1,)), ((), ())), preferred_element_type=F32, precision=precision)


def _sigmoid(z):
    return 1.0 / (1.0 + jnp.exp(-z))


def _tile(n, t):
    t = min(n, t)
    assert n % t == 0, (n, t)
    return t


def _norm_mm_kernel(x_ref, g_ref, w_ref, o_ref, xn_ref):
    @pl.when(pl.program_id(1) == 0)
    def _():
        xn_ref[...] = _rms(x_ref[...].astype(F32), g_ref[...]).astype(BF16)

    o_ref[...] = _dot(xn_ref[...], w_ref[...]).astype(o_ref.dtype)


def _norm_mm(x, g, w, out_dtype, tm=1024, tn=512):
    m, k = x.shape
    n = w.shape[1]
    tm, tn = _tile(m, tm), _tile(n, tn)
    return pl.pallas_call(
        _norm_mm_kernel,
        grid=(m // tm, n // tn),
        in_specs=[
            pl.BlockSpec((tm, k), lambda i, j: (i, 0)),
            pl.BlockSpec((1, k), lambda i, j: (0, 0)),
            pl.BlockSpec((k, tn), lambda i, j: (0, j)),
        ],
        out_specs=pl.BlockSpec((tm, tn), lambda i, j: (i, j)),
        out_shape=jax.ShapeDtypeStruct((m, n), out_dtype),
        scratch_shapes=[pltpu.VMEM((tm, k), BF16)],
        compiler_params=_cparams("parallel", "arbitrary"),
        name="in_proj",
    )(x, g, w)


def _rope128(t, cos, s1, s2):
    return t * cos + pltpu.roll(t, 96, 1) * s1 + pltpu.roll(t, 32, 1) * s2


def _mla_prep_kernel(p0_ref, p1_ref, p2_ref, cqg_ref, ckvg_ref, wq_ref, wkv_ref, qg_ref, kg_ref,
                     cos_ref, s1_ref, s2_ref, q_out, k_out, v_out):
    p = jnp.concatenate([p0_ref[...], p1_ref[...], p2_ref[...]], axis=1)
    cq = p[:, :Q_LORA].astype(F32)
    ckv = p[:, Q_LORA:Q_LORA + KV_LORA].astype(F32)
    kpe = p[:, Q_LORA + KV_LORA:Q_LORA + KV_LORA + LANES].astype(F32)
    q = _dot(_rms(cq, cqg_ref[...]).astype(BF16), wq_ref[...])
    kv = _dot(_rms(ckv, ckvg_ref[...]).astype(BF16), wkv_ref[...])
    cos, s1, s2 = cos_ref[...], s1_ref[...], s2_ref[...]
    qg, kg = qg_ref[...], kg_ref[...]
    scale = LOG2E / math.sqrt(MLA_QK)
    kpe_sq = jnp.sum(kpe * kpe, axis=-1, keepdims=True)
    kpe_r = _rope128(kpe * kg[:, LANES:], cos, s1, s2)
    nk = MLA_HEADS * MLA_NOPE
    for h in range(MLA_HEADS):
        qh = q[:, h * HEAD_PAD:(h + 1) * HEAD_PAD]
        r = lax.rsqrt(jnp.sum(qh * qh, axis=-1, keepdims=True) / MLA_QK + EPS) * scale
        q_out[:, h * HEAD_PAD:h * HEAD_PAD + LANES] = (qh[:, :LANES] * r * qg[:, :LANES]).astype(BF16)
        q_hi = _rope128(qh[:, LANES:] * qg[:, LANES:], cos, s1, s2) * r
        q_out[:, h * HEAD_PAD + LANES:(h + 1) * HEAD_PAD] = q_hi.astype(BF16)
        kn = kv[:, h * MLA_NOPE:(h + 1) * MLA_NOPE]
        rk = lax.rsqrt((jnp.sum(kn * kn, axis=-1, keepdims=True) + kpe_sq) / MLA_QK + EPS)
        k_out[:, h * HEAD_PAD:h * HEAD_PAD + LANES] = (kn * rk * kg[:, :LANES]).astype(BF16)
        k_out[:, h * HEAD_PAD + LANES:(h + 1) * HEAD_PAD] = (kpe_r * rk).astype(BF16)
        v_out[:, 2 * h * MLA_V:(2 * h + 1) * MLA_V] = kv[:, nk + h * MLA_V:nk + (h + 1) * MLA_V].astype(BF16)
        v_out[:, (2 * h + 1) * MLA_V:(2 * h + 2) * MLA_V] = jnp.ones((kv.shape[0], MLA_V), BF16)


def _mla_prep(proj, mla_col, cqg, ckvg, wq, wkv, qg, kg, rope, seq, tm=512):
    m = proj.shape[0]
    tm = _tile(seq, tm)
    nsb = seq // tm
    pw = MLA_SEG // 3
    pspec = lambda k: pl.BlockSpec((tm, pw), lambda i: (i, mla_col // pw + k))
    full = lambda a: pl.BlockSpec(a.shape, lambda i: (0,) * a.ndim)
    rspec = pl.BlockSpec((tm, LANES), lambda i: (i % nsb, 0))
    hq = MLA_HEADS * HEAD_PAD
    return pl.pallas_call(
        _mla_prep_kernel,
        grid=(m // tm,),
        in_specs=[pspec(0), pspec(1), pspec(2), full(cqg), full(ckvg), full(wq), full(wkv),
                  full(qg), full(kg), rspec, rspec, rspec],
        out_specs=[pl.BlockSpec((tm, hq), lambda i: (i, 0)), pl.BlockSpec((tm, hq), lambda i: (i, 0)),
                   pl.BlockSpec((tm, 2 * MLA_HEADS * MLA_V), lambda i: (i, 0))],
        out_shape=[jax.ShapeDtypeStruct((m, hq), BF16), jax.ShapeDtypeStruct((m, hq), BF16),
                   jax.ShapeDtypeStruct((m, 2 * MLA_HEADS * MLA_V), BF16)],
        compiler_params=_cparams("parallel"),
        name="mla_prep",
    )(proj, proj, proj, cqg, ckvg, wq, wkv, qg, kg, *rope)


def _diff_prep_kernel(q_ref, k_ref, v_ref, qg_ref, kg_ref, cos_ref, s1_ref, s2_ref, q_out, k_out, v_out):
    cos, s1, s2 = cos_ref[...], s1_ref[...], s2_ref[...]
    lane = lax.broadcasted_iota(jnp.int32, (1, LANES), 1)
    lo = lane < DIFF_QK
    nh = q_ref.shape[1] // LANES

    def prep(ref, g, out, scale):
        for h in range(nh):
            x = ref[:, h * LANES:(h + 1) * LANES].astype(F32)
            x2 = x * x
            s_lo = jnp.sum(jnp.where(lo, x2, 0.0), axis=-1, keepdims=True)
            s_hi = jnp.sum(jnp.where(lo, 0.0, x2), axis=-1, keepdims=True)
            r = lax.rsqrt(jnp.where(lo, s_lo, s_hi) / DIFF_QK + EPS)
            out[:, h * LANES:(h + 1) * LANES] = (_rope128(x * r * g, cos, s1, s2) * scale).astype(BF16)

    prep(q_ref, qg_ref[...], q_out, LOG2E / math.sqrt(DIFF_QK))
    prep(k_ref, kg_ref[...], k_out, 1.0)
    for h in range(nh):
        v_out[:, 2 * h * LANES:(2 * h + 1) * LANES] = v_ref[:, h * LANES:(h + 1) * LANES]
        v_out[:, (2 * h + 1) * LANES:(2 * h + 2) * LANES] = jnp.ones((v_ref.shape[0], LANES), BF16)


def _diff_prep(proj, q_col, k_col, v_col, qg, kg, rope, seq, tm=512, tn=512):
    m = proj.shape[0]
    width = DIFF_HEADS * 2 * DIFF_QK
    tm = _tile(seq, tm)
    nsb = seq // tm
    nj = width // tn
    full = lambda a: pl.BlockSpec(a.shape, lambda i, j: (0,) * a.ndim)
    rspec = pl.BlockSpec((tm, LANES), lambda i, j: (i % nsb, 0))
    ospec = pl.BlockSpec((tm, tn), lambda i, j: (i, j))
    return pl.pallas_call(
        _diff_prep_kernel,
        grid=(m // tm, nj),
        in_specs=[pl.BlockSpec((tm, tn), lambda i, j: (i, q_col // tn + j)),
                  pl.BlockSpec((tm, tn), lambda i, j: (i, k_col // tn + j)),
                  pl.BlockSpec((tm, tn), lambda i, j: (i, v_col // tn + j)),
                  full(qg), full(kg), rspec, rspec, rspec],
        out_specs=[ospec, ospec, pl.BlockSpec((tm, 2 * tn), lambda i, j: (i, j))],
        out_shape=[jax.ShapeDtypeStruct((m, width), BF16)] * 2 + [jax.ShapeDtypeStruct((m, 2 * width), BF16)],
        compiler_params=_cparams("parallel", "parallel"),
        name="diff_prep",
    )(proj, proj, proj, qg, kg, *rope)


def _softmax_pv(q, k, v1):
    s = _dot_nt(q, k)
    p = jnp.exp2((s - jnp.max(s, axis=-1, keepdims=True)).astype(BF16))
    o = _dot(p, v1)
    return o[:, :LANES] / o[:, LANES:]


def _mla_attn_kernel(q_ref, k_ref, v_ref, o_ref, *, chains):
    rows = q_ref.shape[0] // chains
    k, v1 = k_ref[...], v_ref[...]
    for c in range(chains):
        sl = slice(c * rows, (c + 1) * rows)
        o_ref[sl, :] = _softmax_pv(q_ref[sl, :], k, v1).astype(o_ref.dtype)


def _attn_rows(seq, score_elems):
    return max(CHAIN_ROWS, min(seq, score_elems // seq))


def _mla_attn(q, k, v1, batch, seq):
    m = q.shape[0]
    tq = _attn_rows(seq, MLA_SCORE_ELEMS)
    chains = tq // CHAIN_ROWS
    nq = seq // tq
    return pl.pallas_call(
        functools.partial(_mla_attn_kernel, chains=chains),
        grid=(batch, MLA_HEADS, nq),
        in_specs=[pl.BlockSpec((tq, HEAD_PAD), lambda b, h, i: (b * nq + i, h)),
                  pl.BlockSpec((seq, HEAD_PAD), lambda b, h, i: (b, h)),
                  pl.BlockSpec((seq, 2 * MLA_V), lambda b, h, i: (b, h))],
        out_specs=pl.BlockSpec((tq, MLA_V), lambda b, h, i: (b * nq + i, h)),
        out_shape=jax.ShapeDtypeStruct((m, MLA_HEADS * MLA_V), BF16),
        compiler_params=_cparams("parallel", "parallel", "arbitrary"),
        name="mla_attn",
    )(q, k, v1)


def _diff_attn_kernel(lam_ref, q_ref, k_ref, v_ref, g_ref, o_ref, *, out_scale, chains):
    rows = q_ref.shape[0] // chains
    k, v1 = k_ref[...], v_ref[...]
    lo = lax.broadcasted_iota(jnp.int32, (1, LANES), 1) < DIFF_QK
    for c in range(chains):
        sl = slice(c * rows, (c + 1) * rows)
        q = q_ref[sl, :]
        zero = jnp.zeros_like(q)
        o1 = _softmax_pv(jnp.where(lo, q, zero), k, v1)
        o2 = _softmax_pv(jnp.where(lo, zero, q), k, v1)
        o = o1 - lam_ref[0] * o2
        o_ref[sl, :] = (_rms(o, g_ref[...]) * out_scale).astype(o_ref.dtype)


def _diff_attn(lam, q, k, v1, sub_g, out_scale, batch, seq):
    m = q.shape[0]
    tq = _attn_rows(seq, DIFF_SCORE_ELEMS)
    chains = tq // CHAIN_ROWS
    nq = seq // tq
    return pl.pallas_call(
        functools.partial(_diff_attn_kernel, out_scale=out_scale, chains=chains),
        grid=(batch, DIFF_HEADS, nq),
        in_specs=[pl.BlockSpec(memory_space=pltpu.SMEM),
                  pl.BlockSpec((tq, LANES), lambda b, h, i: (b * nq + i, h)),
                  pl.BlockSpec((seq, LANES), lambda b, h, i: (b, h)),
                  pl.BlockSpec((seq, 2 * DIFF_V), lambda b, h, i: (b, h)),
                  pl.BlockSpec((1, DIFF_V), lambda b, h, i: (0, 0))],
        out_specs=pl.BlockSpec((tq, DIFF_V), lambda b, h, i: (b * nq + i, h)),
        out_shape=jax.ShapeDtypeStruct((m, DIFF_HEADS * DIFF_V), BF16),
        compiler_params=_cparams("parallel", "parallel", "arbitrary"),
        name="diff_attn",
    )(lam, q, k, v1, sub_g)


def _merge_out_kernel(oa_ref, ob_ref, wa_ref, wb_ref, g0_ref, g1_ref, b_ref, wo_ref, x_ref, o_ref):
    d = wa_ref.shape[1]
    a = _dot(oa_ref[...], wa_ref[...])
    b = _dot(ob_ref[...], wb_ref[...])
    g0 = _sigmoid(g0_ref[...].astype(F32) + b_ref[:, :d])
    g1 = _sigmoid(g1_ref[...].astype(F32) + b_ref[:, d:])
    merged = (g0 * a + g1 * b).astype(BF16)
    o_ref[...] = x_ref[...] + _dot(merged, wo_ref[...])


def _merge_out(oa, ob, wa, wb, proj, gate_col, b_gate, wo, x, tm=256):
    m, d = x.shape
    tm = _tile(m, tm)
    g0b = gate_col // d
    resident = lambda a: pl.BlockSpec(a.shape, lambda i: (0,) * a.ndim, pipeline_mode=pl.Buffered(1))
    row = lambda a: pl.BlockSpec((tm, a.shape[1]), lambda i: (i, 0))
    return pl.pallas_call(
        _merge_out_kernel,
        grid=(m // tm,),
        in_specs=[row(oa), row(ob), resident(wa), resident(wb),
                  pl.BlockSpec((tm, d), lambda i: (i, g0b)),
                  pl.BlockSpec((tm, d), lambda i: (i, g0b + 1)),
                  resident(b_gate), resident(wo), row(x)],
        out_specs=pl.BlockSpec((tm, d), lambda i: (i, 0)),
        out_shape=jax.ShapeDtypeStruct((m, d), F32),
        compiler_params=_cparams("parallel"),
        name="merge_out",
    )(oa, ob, wa, wb, proj, proj, b_gate, wo, x)


def _router_kernel(x_ref, g_ref, wr_ref, h_ref, aff_ref):
    xn = _rms(x_ref[...], g_ref[...])
    h_ref[...] = xn.astype(BF16)
    lg = _dot_nt(wr_ref[...], xn, precision=lax.Precision.HIGHEST)
    e = jnp.exp(lg - jnp.max(lg, axis=0, keepdims=True))
    aff_ref[...] = e / jnp.sum(e, axis=0, keepdims=True)


def _router(x, g, wr_t, tm=512):
    m, d = x.shape
    ne = wr_t.shape[0]
    tm = _tile(m, tm)
    return pl.pallas_call(
        _router_kernel,
        grid=(m // tm,),
        in_specs=[pl.BlockSpec((tm, d), lambda i: (i, 0)),
                  pl.BlockSpec((1, d), lambda i: (0, 0)),
                  pl.BlockSpec((ne, d), lambda i: (0, 0))],
        out_specs=[pl.BlockSpec((tm, d), lambda i: (i, 0)), pl.BlockSpec((ne, tm), lambda i: (0, i))],
        out_shape=[jax.ShapeDtypeStruct((m, d), BF16), jax.ShapeDtypeStruct((ne, m), F32)],
        compiler_params=_cparams("parallel"),
        name="router",
    )(x, g, wr_t)


def _select_kernel(aff_ref, rank_ref, cum_ref, *, cap):
    ne, n = aff_ref.shape
    nch = n // LANES

    def count(mask):
        return jnp.sum(jnp.where(mask, 1.0, 0.0), axis=1, keepdims=True)

    def bisect(i, thr):
        cand = thr | jnp.left_shift(jnp.int32(1), 30 - i)
        bits = pltpu.bitcast(aff_ref[...], jnp.int32)
        return jnp.where(count(bits >= cand) >= cap, cand, thr)

    thr = lax.fori_loop(0, 31, bisect, jnp.zeros((ne, 1), jnp.int32))
    need = cap - count(pltpu.bitcast(aff_ref[...], jnp.int32) > thr)
    tri = jnp.where(lax.broadcasted_iota(jnp.int32, (LANES, LANES), 0)
                    <= lax.broadcasted_iota(jnp.int32, (LANES, LANES), 1), 1.0, 0.0).astype(BF16)

    def chunk(j, carry):
        c_eq, c_sel = carry
        off = pl.multiple_of(j * LANES, LANES)
        bits = pltpu.bitcast(aff_ref[:, pl.ds(off, LANES)], jnp.int32)
        eq = jnp.where(bits == thr, 1.0, 0.0)
        eq_incl = _dot(eq.astype(BF16), tri)
        take = jnp.where(c_eq + eq_incl - eq < need, eq, 0.0)
        sel = jnp.where(bits > thr, 1.0, take)
        sel_incl = _dot(sel.astype(BF16), tri)
        rank = c_sel + sel_incl - sel
        rank_ref[:, pl.ds(off, LANES)] = jnp.where(sel > 0.0, rank, -1.0).astype(jnp.int32)
        cum_ref[j] = jnp.broadcast_to(c_sel, (ne, LANES)).astype(jnp.int32)
        return c_eq + eq_incl[:, LANES - 1:], c_sel + sel_incl[:, LANES - 1:]

    zero = jnp.zeros((ne, 1), F32)
    lax.fori_loop(0, nch, chunk, (zero, zero))


def _select(aff_t, cap):
    ne, n = aff_t.shape
    nch = n // LANES
    return pl.pallas_call(
        functools.partial(_select_kernel, cap=cap),
        grid=(1,),
        in_specs=[pl.BlockSpec((ne, n), lambda i: (0, 0))],
        out_specs=[pl.BlockSpec((ne, n), lambda i: (0, 0)), pl.BlockSpec((nch, ne, LANES), lambda i: (0, 0, 0))],
        out_shape=[jax.ShapeDtypeStruct((ne, n), jnp.int32), jax.ShapeDtypeStruct((nch, ne, LANES), jnp.int32)],
        compiler_params=_cparams("arbitrary"),
        name="select",
    )(aff_t)


SUB = 128
ROW_TILE = 16


def _block_starts(cum, cap, tb):
    r = cum[::tb // LANES, :, 0]
    r = jnp.concatenate([r, jnp.full((1, r.shape[1]), cap, jnp.int32)], axis=0)
    return r.T.reshape(-1)


def _dispatch_kernel(r_ref, h_ref, rank_ref, xs_ref, stage, carry, sems, *, nb, cap_pad):
    b = pl.program_id(0)
    ne, tb = rank_ref.shape
    sub_iota = lax.broadcasted_iota(jnp.int32, (SUB, tb), 0)

    def copy(e, row):
        return pltpu.make_async_copy(stage.at[e], xs_ref.at[pl.ds(row, SUB)], sems.at[e])

    @pl.when(b == 0)
    def _():
        carry[...] = jnp.zeros_like(carry)
        for e in range(ne):
            stage[e] = jnp.zeros(stage.shape[1:], BF16)
            pad = copy(e, (e + 1) * cap_pad - SUB)
            pad.start()
            pad.wait()

    def expert(e, _):
        r0 = r_ref[e * (nb + 1) + b]
        r1 = r_ref[e * (nb + 1) + b + 1]
        a = (r0 // ROW_TILE) * ROW_TILE
        nsub = (r1 - a) // SUB + 1
        rank_row = rank_ref[pl.ds(e, 1), :]

        def sub(s, _):
            @pl.when((b > 0) | (s > 0))
            def _():
                copy(e, 0).wait()

            base = a + s * SUB
            q = jnp.where(sub_iota == rank_row - base, 1.0, 0.0).astype(BF16)
            rows = _dot(q, h_ref[...])
            stage[e] = rows.astype(BF16)

            @pl.when(s == 0)
            def _():
                stage[e, pl.ds(0, ROW_TILE), :] = (rows[:ROW_TILE] + carry[e]).astype(BF16)

            @pl.when(s == nsub - 1)
            def _():
                nxt = pl.multiple_of((r1 // ROW_TILE) * ROW_TILE - base, ROW_TILE)
                carry[e] = stage[e, pl.ds(nxt, ROW_TILE), :].astype(F32)

            copy(e, pl.multiple_of(e * cap_pad + base, ROW_TILE)).start()
            return 0

        lax.fori_loop(0, nsub, sub, 0)
        return 0

    lax.fori_loop(0, ne, expert, 0)

    @pl.when(b == nb - 1)
    def _():
        for e in range(ne):
            copy(e, 0).wait()


def _dispatch(r, h2, rank, cap, tb):
    n, d = h2.shape
    ne = rank.shape[0]
    nb = n // tb
    cap_pad = cap + SUB
    return pl.pallas_call(
        functools.partial(_dispatch_kernel, nb=nb, cap_pad=cap_pad),
        grid_spec=pltpu.PrefetchScalarGridSpec(
            num_scalar_prefetch=1,
            grid=(nb,),
            in_specs=[pl.BlockSpec((tb, d), lambda i, r: (i, 0)), pl.BlockSpec((ne, tb), lambda i, r: (0, i))],
            out_specs=pl.BlockSpec(memory_space=pl.ANY),
            scratch_shapes=[pltpu.VMEM((ne, SUB, d), BF16), pltpu.VMEM((ne, ROW_TILE, d), F32),
                            pltpu.SemaphoreType.DMA((ne,))],
        ),
        out_shape=jax.ShapeDtypeStruct((ne * cap_pad, d), BF16),
        compiler_params=_cparams("arbitrary"),
        name="dispatch",
    )(r, h2, rank)


def _ffn_kernel(x_ref, wg_ref, wu_ref, wd_ref, o_ref, acc_ref, *, chains):
    f = pl.program_id(2)
    rows = x_ref.shape[1] // chains

    @pl.when(f == 0)
    def _():
        acc_ref[...] = jnp.zeros_like(acc_ref)

    wg, wu, wd = wg_ref[0].astype(BF16), wu_ref[0].astype(BF16), wd_ref[0].astype(BF16)
    for c in range(chains):
        sl = slice(c * rows, (c + 1) * rows)
        x = x_ref[0, sl, :]
        hg = _dot(x, wg)
        hu = _dot(x, wu)
        hid = (hg * _sigmoid(hg) * hu).astype(BF16)
        acc_ref[sl, :] += _dot(hid, wd)

    @pl.when(f == pl.num_programs(2) - 1)
    def _():
        o_ref[...] = acc_ref[...].astype(o_ref.dtype)


def _ffn(xs, layer, wg, wu, wd, cap, t=1024, tf=512, chains=1):
    _, ne, d, fdim = wg.shape
    t, tf = _tile(cap, t), _tile(fdim, tf)
    nt = cap // t
    xs3 = xs.reshape(ne, cap + SUB, d)
    return pl.pallas_call(
        functools.partial(_ffn_kernel, chains=chains),
        grid=(ne, nt, fdim // tf),
        in_specs=[pl.BlockSpec((1, t, d), lambda e, i, f: (e, i, 0)),
                  pl.BlockSpec((None, 1, d, tf), lambda e, i, f: (layer, e, 0, f)),
                  pl.BlockSpec((None, 1, d, tf), lambda e, i, f: (layer, e, 0, f)),
                  pl.BlockSpec((None, 1, tf, d), lambda e, i, f: (layer, e, f, 0))],
        out_specs=pl.BlockSpec((t, d), lambda e, i, f: (e * nt + i, 0)),
        out_shape=jax.ShapeDtypeStruct((ne * cap, d), BF16),
        scratch_shapes=[pltpu.VMEM((t, d), F32)],
        compiler_params=_cparams("parallel", "parallel", "arbitrary"),
        name="expert_ffn",
    )(xs3, wg, wu, wd)


def _combine_kernel(r_ref, x_ref, rank_ref, aff_ref, y_ref, o_ref, ybuf, xbuf, sems, xsem, *, nb, cap):
    b = pl.program_id(0)
    ne, tb = rank_ref.shape
    total = ne * cap
    sub_iota = lax.broadcasted_iota(jnp.int32, (SUB, tb), 0)

    def first_row(bb, e):
        return ((e * cap + r_ref[e * (nb + 1) + bb]) // ROW_TILE) * ROW_TILE

    def fetch(bb, e, slot):
        row = pl.multiple_of(jnp.minimum(first_row(bb, e), total - SUB), ROW_TILE)
        return pltpu.make_async_copy(y_ref.at[pl.ds(row, SUB)], ybuf.at[slot, e // 2, pl.ds((e % 2) * SUB, SUB)],
                                     sems.at[slot, e])

    @pl.when(b == 0)
    def _():
        for e in range(ne):
            fetch(0, e, 0).start()

    @pl.when(b + 1 < nb)
    def _():
        for e in range(ne):
            fetch(b + 1, e, (b + 1) % 2).start()

    slot = b % 2
    o_ref[...] = x_ref[...]

    def weights(e, start, lower):
        rank_row = rank_ref[pl.ds(e, 1), :]
        row = rank_row + e * cap
        gate = jnp.where(rank_row >= 0, jnp.where(row >= lower, aff_ref[pl.ds(e, 1), :], 0.0), 0.0)
        return jnp.where(sub_iota == row - start, gate, 0.0).astype(BF16)

    def scatter(w, rows):
        return lax.dot_general(w, rows, (((0,), (0,)), ((), ())), preferred_element_type=F32)

    for pair in range(ne // 2):
        ws = []
        for e in (2 * pair, 2 * pair + 1):
            fetch(b, e, slot).wait()
            a = first_row(b, e)
            ws.append(weights(e, jnp.minimum(a, total - SUB), a))
        o_ref[...] += scatter(jnp.concatenate(ws, axis=0), ybuf[slot, pair])

    def extra(e, _):
        a = first_row(b, e)
        r1 = e * cap + r_ref[e * (nb + 1) + b + 1]

        def sub(s, _):
            lower = a + s * SUB
            start = pl.multiple_of(jnp.minimum(lower, total - SUB), ROW_TILE)
            cp = pltpu.make_async_copy(y_ref.at[pl.ds(start, SUB)], xbuf, xsem)
            cp.start()
            cp.wait()
            o_ref[...] += scatter(weights(e, start, lower), xbuf[...])
            return 0

        lax.fori_loop(1, (r1 - a + SUB - 1) // SUB, sub, 0)
        return 0

    lax.fori_loop(0, ne, extra, 0)


def _combine(r, x, rank, aff_t, y, cap, tb):
    n, d = x.shape
    ne = rank.shape[0]
    nb = n // tb
    return pl.pallas_call(
        functools.partial(_combine_kernel, nb=nb, cap=cap),
        grid_spec=pltpu.PrefetchScalarGridSpec(
            num_scalar_prefetch=1,
            grid=(nb,),
            in_specs=[pl.BlockSpec((tb, d), lambda i, r: (i, 0)),
                      pl.BlockSpec((ne, tb), lambda i, r: (0, i)),
                      pl.BlockSpec((ne, tb), lambda i, r: (0, i)),
                      pl.BlockSpec(memory_space=pl.ANY)],
            out_specs=pl.BlockSpec((tb, d), lambda i, r: (i, 0)),
            scratch_shapes=[pltpu.VMEM((2, ne // 2, 2 * SUB, d), BF16), pltpu.VMEM((SUB, d), BF16),
                            pltpu.SemaphoreType.DMA((2, ne)), pltpu.SemaphoreType.DMA(())],
        ),
        out_shape=jax.ShapeDtypeStruct((n, d), F32),
        compiler_params=_cparams("arbitrary"),
        name="combine",
    )(r, x, rank, aff_t, y)


def _moe(x, g, wr_t, layer, wg, wu, wd, tb=512):
    n = x.shape[0]
    ne = wr_t.shape[0]
    cap = EC_CAPACITY_FACTOR * n // ne
    tb = _tile(n, tb)
    h2, aff_t = _router(x, g, wr_t)
    rank, cum = _select(aff_t, cap)
    r = _block_starts(cum, cap, tb)
    xs = _dispatch(r, h2, rank, cap, tb)
    y = _ffn(xs, layer, wg, wu, wd, cap)
    return _combine(r, x, rank, aff_t, y, cap, tb)


def _rope_tables(seq):
    def tables(dim, reps):
        inv = 1.0 / (ROPE_THETA ** (jnp.arange(0, dim, 2, dtype=F32) / dim))
        ang = jnp.arange(seq, dtype=F32)[:, None] * inv[None, :]
        c, s = jnp.cos(ang), jnp.sin(ang)
        z = jnp.zeros_like(c)
        pad = jnp.zeros((seq, LANES - reps * dim), F32)
        cos = jnp.concatenate([c, c] * reps + [pad], axis=-1)
        s1 = jnp.concatenate([-s, z] * reps + [pad], axis=-1)
        s2 = jnp.concatenate([z, s] * reps + [pad], axis=-1)
        return cos, s1, s2

    return tables(MLA_ROPE, 1), tables(DIFF_QK, LANES // DIFF_QK)


def _prep_layer(l, p):
    d = p["w_in"].shape[1]
    w_in = p["w_in"][l]
    o_cq, o_ckv, o_kpe = 0, Q_LORA, Q_LORA + KV_LORA
    o_qd = o_kpe + MLA_ROPE
    dw = DIFF_HEADS * 2 * DIFF_QK
    o_kd, o_vd = o_qd + dw, o_qd + 2 * dw
    o_g = o_vd + DIFF_HEADS * DIFF_V
    w_in_p = jnp.concatenate(
        [w_in[:, o_g:], w_in[:, o_qd:o_g], w_in[:, :o_qd], jnp.zeros((d, MLA_SEG - o_qd), F32)], axis=1).astype(BF16)
    cols = dict(gate=0, qd=2 * d, kd=2 * d + dw, vd=2 * d + 2 * dw, mla=2 * d + 3 * dw)
    wq = p["w_q_up"][l].reshape(Q_LORA, MLA_HEADS, MLA_QK)
    wq = jnp.pad(wq, ((0, 0), (0, 0), (0, HEAD_PAD - MLA_QK))).reshape(Q_LORA, MLA_HEADS * HEAD_PAD).astype(BF16)
    wkv = p["w_kv_up"][l].reshape(KV_LORA, MLA_HEADS, MLA_NOPE + MLA_V)
    wkv = jnp.concatenate([wkv[:, :, :MLA_NOPE].reshape(KV_LORA, -1), wkv[:, :, MLA_NOPE:].reshape(KV_LORA, -1)],
                          axis=1).astype(BF16)
    row = lambda a: a.reshape(1, -1).astype(F32)
    pad_g = lambda a: jnp.pad(a, (0, HEAD_PAD - MLA_QK)).reshape(1, -1)
    lam_init = 0.8 - 0.6 * math.exp(-0.3 * l)
    lam = (jnp.exp(jnp.sum(p["lambda_q1"][l].astype(F32) * p["lambda_k1"][l].astype(F32)))
           - jnp.exp(jnp.sum(p["lambda_q2"][l].astype(F32) * p["lambda_k2"][l].astype(F32))) + lam_init)
    return dict(
        cols=cols, w_in=w_in_p, norm_attn_g=row(p["norm_attn_g"][l]), b_gate=row(p["b_gate"][l]),
        cq_g=row(p["mla_cq_g"][l]), ckv_g=row(p["mla_ckv_g"][l]), wq=wq, wkv=wkv,
        q_g=pad_g(p["mla_q_g"][l]), k_g=pad_g(p["mla_k_g"][l]),
        dq_g=jnp.tile(p["diff_q_g"][l], LANES // DIFF_QK).reshape(1, -1),
        dk_g=jnp.tile(p["diff_k_g"][l], LANES // DIFF_QK).reshape(1, -1),
        lam=lam.reshape(1).astype(F32), out_scale=1.0 - lam_init, sub_g=row(p["diff_sub_g"][l]),
        wa=p["w_a_out"][l].astype(BF16), wb=p["w_b_out"][l].astype(BF16), wo=p["w_o"][l].astype(BF16),
        norm_ffn_g=row(p["norm_ffn_g"][l]), wr_t=p["w_router"][l].T.astype(F32),
        layer=l, wg=p["w_e_gate"], wu=p["w_e_up"], wd=p["w_e_down"],
    )


def _layer(x, lp, ropes, batch, seq):
    m, d = x.shape
    rope_a, rope_b = ropes
    cols = lp["cols"]
    proj = _norm_mm(x, lp["norm_attn_g"], lp["w_in"], BF16)
    q_a, k_a, v_a = _mla_prep(proj, cols["mla"], lp["cq_g"], lp["ckv_g"], lp["wq"], lp["wkv"], lp["q_g"], lp["k_g"],
                              rope_a, seq)
    o_a = _mla_attn(q_a, k_a, v_a, batch, seq)
    q_d, k_d, v_d = _diff_prep(proj, cols["qd"], cols["kd"], cols["vd"], lp["dq_g"], lp["dk_g"], rope_b, seq)
    o_b = _diff_attn(lp["lam"], q_d, k_d, v_d, lp["sub_g"], lp["out_scale"], batch, seq)
    x = _merge_out(o_a, o_b, lp["wa"], lp["wb"], proj, cols["gate"], lp["b_gate"], lp["wo"], x)
    return _moe(x, lp["norm_ffn_g"], lp["wr_t"], lp["layer"], lp["wg"], lp["wu"], lp["wd"])


def kernel(x_prompt, x_sample, norm_attn_g, w_in, b_gate, mla_cq_g, w_q_up, mla_ckv_g, w_kv_up, mla_q_g, mla_k_g,
           w_a_out, diff_q_g, diff_k_g, lambda_q1, lambda_k1, lambda_q2, lambda_k2, diff_sub_g, w_b_out, w_o,
           norm_ffn_g, w_router, w_e_gate, w_e_up, w_e_down):
    p = dict(norm_attn_g=norm_attn_g, w_in=w_in, b_gate=b_gate, mla_cq_g=mla_cq_g, w_q_up=w_q_up,
             mla_ckv_g=mla_ckv_g, w_kv_up=w_kv_up, mla_q_g=mla_q_g, mla_k_g=mla_k_g, w_a_out=w_a_out,
             diff_q_g=diff_q_g, diff_k_g=diff_k_g, lambda_q1=lambda_q1, lambda_k1=lambda_k1, lambda_q2=lambda_q2,
             lambda_k2=lambda_k2, diff_sub_g=diff_sub_g, w_b_out=w_b_out, w_o=w_o, norm_ffn_g=norm_ffn_g,
             w_router=w_router, w_e_gate=w_e_gate, w_e_up=w_e_up, w_e_down=w_e_down)
    depth = w_in.shape[0]
    groups = [x_prompt, x_sample]
    shapes = [g.shape for g in groups]
    xs = [g.reshape(-1, g.shape[-1]) for g in groups]
    ropes = [_rope_tables(s[1]) for s in shapes]
    for l in range(depth):
        lp = _prep_layer(l, p)
        xs = [_layer(x, lp, r, s[0], s[1]) for x, r, s in zip(xs, ropes, shapes)]
    return tuple(x.reshape(s) for x, s in zip(xs, shapes))
```

```python
import functools
import math

import jax
import jax.numpy as jnp
from jax import lax
from jax.experimental import pallas as pl
from jax.experimental.pallas import tpu as pltpu

F32 = jnp.float32
BF16 = jnp.bfloat16

MLA_HEADS = 8
MLA_NOPE = 128
MLA_ROPE = 64
MLA_QK = MLA_NOPE + MLA_ROPE
MLA_V = 128
Q_LORA = 768
KV_LORA = 512
DIFF_HEADS = 8
DIFF_QK = 64
DIFF_V = 128
N_EXPERTS = 16
EC_CAPACITY_FACTOR = 2
ROPE_THETA = 10000.0
EPS = 1e-6

LOG2E = 1.4426950408889634
LANES = 128
HEAD_PAD = 2 * LANES
MLA_SEG = 1536
VMEM_LIMIT = 56 * 1024 * 1024
CHAIN_ROWS = 256
MLA_SCORE_ELEMS = 2048 * 4096
DIFF_SCORE_ELEMS = 2048 * 2048


def _cparams(*sem):
    return pltpu.CompilerParams(dimension_semantics=sem, vmem_limit_bytes=VMEM_LIMIT)


def _rms(x, g):
    ms = jnp.mean(x * x, axis=-1, keepdims=True)
    return x * lax.rsqrt(ms + EPS) * g


def _dot(a, b):
    return jnp.dot(a, b, preferred_element_type=F32)


def _dot_nt(a, b, precision=None):
    return lax.dot_general(a, b, (((1,), (1,)), ((), ())), preferred_element_type=F32, precision=precision)


def _sigmoid(z):
    return 1.0 / (1.0 + jnp.exp(-z))


def _tile(n, t):
    t = min(n, t)
    assert n % t == 0, (n, t)
    return t


def _norm_mm_kernel(x_ref, g_ref, w_ref, o_ref, xn_ref):
    @pl.when(pl.program_id(1) == 0)
    def _():
        xn_ref[...] = _rms(x_ref[...].astype(F32), g_ref[...]).astype(BF16)

    o_ref[...] = _dot(xn_ref[...], w_ref[...]).astype(o_ref.dtype)


def _norm_mm(x, g, w, out_dtype, tm=1024, tn=512):
    m, k = x.shape
    n = w.shape[1]
    tm, tn = _tile(m, tm), _tile(n, tn)
    return pl.pallas_call(
        _norm_mm_kernel,
        grid=(m // tm, n // tn),
        in_specs=[
            pl.BlockSpec((tm, k), lambda i, j: (i, 0)),
            pl.BlockSpec((1, k), lambda i, j: (0, 0)),
            pl.BlockSpec((k, tn), lambda i, j: (0, j)),
        ],
        out_specs=pl.BlockSpec((tm, tn), lambda i, j: (i, j)),
        out_shape=jax.ShapeDtypeStruct((m, n), out_dtype),
        scratch_shapes=[pltpu.VMEM((tm, k), BF16)],
        compiler_params=_cparams("parallel", "arbitrary"),
        name="in_proj",
    )(x, g, w)


def _rope128(t, cos, s1, s2):
    return t * cos + pltpu.roll(t, 96, 1) * s1 + pltpu.roll(t, 32, 1) * s2


def _mla_prep_kernel(p0_ref, p1_ref, p2_ref, cqg_ref, ckvg_ref, wq_ref, wkv_ref, qg_ref, kg_ref,
                     cos_ref, s1_ref, s2_ref, q_out, k_out, v_out):
    p = jnp.concatenate([p0_ref[...], p1_ref[...], p2_ref[...]], axis=1)
    cq = p[:, :Q_LORA].astype(F32)
    ckv = p[:, Q_LORA:Q_LORA + KV_LORA].astype(F32)
    kpe = p[:, Q_LORA + KV_LORA:Q_LORA + KV_LORA + LANES].astype(F32)
    q = _dot(_rms(cq, cqg_ref[...]).astype(BF16), wq_ref[...])
    kv = _dot(_rms(ckv, ckvg_ref[...]).astype(BF16), wkv_ref[...])
    cos, s1, s2 = cos_ref[...], s1_ref[...], s2_ref[...]
    qg, kg = qg_ref[...], kg_ref[...]
    scale = LOG2E / math.sqrt(MLA_QK)
    kpe_sq = jnp.sum(kpe * kpe, axis=-1, keepdims=True)
    kpe_r = _rope128(kpe * kg[:, LANES:], cos, s1, s2)
    nk = MLA_HEADS * MLA_NOPE
    for h in range(MLA_HEADS):
        qh = q[:, h * HEAD_PAD:(h + 1) * HEAD_PAD]
        r = lax.rsqrt(jnp.sum(qh * qh, axis=-1, keepdims=True) / MLA_QK + EPS) * scale
        q_out[:, h * HEAD_PAD:h * HEAD_PAD + LANES] = (qh[:, :LANES] * r * qg[:, :LANES]).astype(BF16)
        q_hi = _rope128(qh[:, LANES:] * qg[:, LANES:], cos, s1, s2) * r
        q_out[:, h * HEAD_PAD + LANES:(h + 1) * HEAD_PAD] = q_hi.astype(BF16)
        kn = kv[:, h * MLA_NOPE:(h + 1) * MLA_NOPE]
        rk = lax.rsqrt((jnp.sum(kn * kn, axis=-1, keepdims=True) + kpe_sq) / MLA_QK + EPS)
        k_out[:, h * HEAD_PAD:h * HEAD_PAD + LANES] = (kn * rk * kg[:, :LANES]).astype(BF16)
        k_out[:, h * HEAD_PAD + LANES:(h + 1) * HEAD_PAD] = (kpe_r * rk).astype(BF16)
        v_out[:, 2 * h * MLA_V:(2 * h + 1) * MLA_V] = kv[:, nk + h * MLA_V:nk + (h + 1) * MLA_V].astype(BF16)
        v_out[:, (2 * h + 1) * MLA_V:(2 * h + 2) * MLA_V] = jnp.ones((kv.shape[0], MLA_V), BF16)


def _mla_prep(proj, mla_col, cqg, ckvg, wq, wkv, qg, kg, rope, seq, tm=512):
    m = proj.shape[0]
    tm = _tile(seq, tm)
    nsb = seq // tm
    pw = MLA_SEG // 3
    pspec = lambda k: pl.BlockSpec((tm, pw), lambda i: (i, mla_col // pw + k))
    full = lambda a: pl.BlockSpec(a.shape, lambda i: (0,) * a.ndim)
    rspec = pl.BlockSpec((tm, LANES), lambda i: (i % nsb, 0))
    hq = MLA_HEADS * HEAD_PAD
    return pl.pallas_call(
        _mla_prep_kernel,
        grid=(m // tm,),
        in_specs=[pspec(0), pspec(1), pspec(2), full(cqg), full(ckvg), full(wq), full(wkv),
                  full(qg), full(kg), rspec, rspec, rspec],
        out_specs=[pl.BlockSpec((tm, hq), lambda i: (i, 0)), pl.BlockSpec((tm, hq), lambda i: (i, 0)),
                   pl.BlockSpec((tm, 2 * MLA_HEADS * MLA_V), lambda i: (i, 0))],
        out_shape=[jax.ShapeDtypeStruct((m, hq), BF16), jax.ShapeDtypeStruct((m, hq), BF16),
                   jax.ShapeDtypeStruct((m, 2 * MLA_HEADS * MLA_V), BF16)],
        compiler_params=_cparams("parallel"),
        name="mla_prep",
    )(proj, proj, proj, cqg, ckvg, wq, wkv, qg, kg, *rope)


def _diff_prep_kernel(q_ref, k_ref, v_ref, qg_ref, kg_ref, cos_ref, s1_ref, s2_ref, q_out, k_out, v_out):
    cos, sin = cos_ref[...], s2_ref[...] - s1_ref[...]
    nh = q_ref.shape[1] // LANES
    i = lax.broadcasted_iota(jnp.int32, (LANES, LANES), 0)
    j = lax.broadcasted_iota(jnp.int32, (LANES, LANES), 1)
    half = DIFF_QK // 2
    first = (j % DIFF_QK) < half
    same = jnp.where(i // DIFF_QK == j // DIFF_QK, 1.0, 0.0).astype(BF16)
    perm = jnp.where(i == j + half, jnp.where(first, -1.0, 0.0),
                     jnp.where(i == j - half, jnp.where(first, 0.0, 1.0), 0.0)).astype(BF16)

    def prep(ref, g_ref, out, scale):
        t1 = g_ref[0:1, :] * cos
        t2 = g_ref[1:2, :] * sin
        for h in range(nh):
            xb = ref[:, h * LANES:(h + 1) * LANES]
            x = xb.astype(F32)
            ssq = _dot((x * x).astype(BF16), same)
            r = lax.rsqrt(ssq / DIFF_QK + EPS) * scale
            out[:, h * LANES:(h + 1) * LANES] = ((x * t1 + _dot(xb, perm) * t2) * r).astype(BF16)

    prep(q_ref, qg_ref, q_out, LOG2E / math.sqrt(DIFF_QK))
    prep(k_ref, kg_ref, k_out, 1.0)
    for h in range(nh):
        v_out[:, 2 * h * LANES:(2 * h + 1) * LANES] = v_ref[:, h * LANES:(h + 1) * LANES]
        v_out[:, (2 * h + 1) * LANES:(2 * h + 2) * LANES] = jnp.ones((v_ref.shape[0], LANES), BF16)


def _diff_prep(proj, q_col, k_col, v_col, qg, kg, rope, seq, tm=512, tn=512):
    m = proj.shape[0]
    width = DIFF_HEADS * 2 * DIFF_QK
    tm = _tile(seq, tm)
    nsb = seq // tm
    nj = width // tn
    full = lambda a: pl.BlockSpec(a.shape, lambda i, j: (0,) * a.ndim)
    rspec = pl.BlockSpec((tm, LANES), lambda i, j: (i % nsb, 0))
    ospec = pl.BlockSpec((tm, tn), lambda i, j: (i, j))
    return pl.pallas_call(
        _diff_prep_kernel,
        grid=(m // tm, nj),
        in_specs=[pl.BlockSpec((tm, tn), lambda i, j: (i, q_col // tn + j)),
                  pl.BlockSpec((tm, tn), lambda i, j: (i, k_col // tn + j)),
                  pl.BlockSpec((tm, tn), lambda i, j: (i, v_col // tn + j)),
                  full(qg), full(kg), rspec, rspec, rspec],
        out_specs=[ospec, ospec, pl.BlockSpec((tm, 2 * tn), lambda i, j: (i, j))],
        out_shape=[jax.ShapeDtypeStruct((m, width), BF16)] * 2 + [jax.ShapeDtypeStruct((m, 2 * width), BF16)],
        compiler_params=_cparams("parallel", "parallel"),
        name="diff_prep",
    )(proj, proj, proj, qg, kg, *rope)


def _softmax_pv(q, k, v1):
    s = _dot_nt(q, k)
    p = jnp.exp2((s - jnp.max(s, axis=-1, keepdims=True)).astype(BF16))
    o = _dot(p, v1)
    return o[:, :LANES] / o[:, LANES:]


def _mla_attn_kernel(q_ref, k_ref, v_ref, o_ref, *, chains):
    rows = q_ref.shape[0] // chains
    k, v1 = k_ref[...], v_ref[...]
    for c in range(chains):
        sl = slice(c * rows, (c + 1) * rows)
        o_ref[sl, :] = _softmax_pv(q_ref[sl, :], k, v1).astype(o_ref.dtype)


def _attn_rows(seq, score_elems):
    return max(CHAIN_ROWS, min(seq, score_elems // seq))


def _mla_attn(q, k, v1, batch, seq):
    m = q.shape[0]
    tq = _attn_rows(seq, MLA_SCORE_ELEMS)
    chains = tq // CHAIN_ROWS
    nq = seq // tq
    return pl.pallas_call(
        functools.partial(_mla_attn_kernel, chains=chains),
        grid=(batch, MLA_HEADS, nq),
        in_specs=[pl.BlockSpec((tq, HEAD_PAD), lambda b, h, i: (b * nq + i, h)),
                  pl.BlockSpec((seq, HEAD_PAD), lambda b, h, i: (b, h)),
                  pl.BlockSpec((seq, 2 * MLA_V), lambda b, h, i: (b, h))],
        out_specs=pl.BlockSpec((tq, MLA_V), lambda b, h, i: (b * nq + i, h)),
        out_shape=jax.ShapeDtypeStruct((m, MLA_HEADS * MLA_V), BF16),
        compiler_params=_cparams("parallel", "parallel", "arbitrary"),
        name="mla_attn",
    )(q, k, v1)


def _diff_attn_kernel(lam_ref, q_ref, k_ref, v_ref, g_ref, o_ref, *, out_scale, chains):
    rows = q_ref.shape[0] // chains
    k, v1 = k_ref[...], v_ref[...]
    lo = lax.broadcasted_iota(jnp.int32, (1, LANES), 1) < DIFF_QK
    for c in range(chains):
        sl = slice(c * rows, (c + 1) * rows)
        q = q_ref[sl, :]
        zero = jnp.zeros_like(q)
        o1 = _softmax_pv(jnp.where(lo, q, zero), k, v1)
        o2 = _softmax_pv(jnp.where(lo, zero, q), k, v1)
        o = o1 - lam_ref[0] * o2
        o_ref[sl, :] = (_rms(o, g_ref[...]) * out_scale).astype(o_ref.dtype)


def _diff_attn(lam, q, k, v1, sub_g, out_scale, batch, seq):
    m = q.shape[0]
    tq = _attn_rows(seq, DIFF_SCORE_ELEMS)
    chains = tq // CHAIN_ROWS
    nq = seq // tq
    return pl.pallas_call(
        functools.partial(_diff_attn_kernel, out_scale=out_scale, chains=chains),
        grid=(batch, DIFF_HEADS, nq),
        in_specs=[pl.BlockSpec(memory_space=pltpu.SMEM),
                  pl.BlockSpec((tq, LANES), lambda b, h, i: (b * nq + i, h)),
                  pl.BlockSpec((seq, LANES), lambda b, h, i: (b, h)),
                  pl.BlockSpec((seq, 2 * DIFF_V), lambda b, h, i: (b, h)),
                  pl.BlockSpec((1, DIFF_V), lambda b, h, i: (0, 0))],
        out_specs=pl.BlockSpec((tq, DIFF_V), lambda b, h, i: (b * nq + i, h)),
        out_shape=jax.ShapeDtypeStruct((m, DIFF_HEADS * DIFF_V), BF16),
        compiler_params=_cparams("parallel", "parallel", "arbitrary"),
        name="diff_attn",
    )(lam, q, k, v1, sub_g)


def _merge_out_kernel(oa_ref, ob_ref, wa_ref, wb_ref, g0_ref, g1_ref, b_ref, wo_ref, x_ref, ng_ref, wr_ref,
                      o_ref, h_ref, aff_ref):
    d = wa_ref.shape[1]
    a = _dot(oa_ref[...], wa_ref[...])
    b = _dot(ob_ref[...], wb_ref[...])
    g0 = _sigmoid(g0_ref[...].astype(F32) + b_ref[:, :d])
    g1 = _sigmoid(g1_ref[...].astype(F32) + b_ref[:, d:])
    merged = (g0 * a + g1 * b).astype(BF16)
    x = x_ref[...] + _dot(merged, wo_ref[...])
    o_ref[...] = x
    xn = _rms(x, ng_ref[...])
    h_ref[...] = xn.astype(BF16)
    lg = _dot_nt(wr_ref[...], xn, precision=lax.Precision.HIGHEST)
    e = jnp.exp(lg - jnp.max(lg, axis=0, keepdims=True))
    aff_ref[...] = e / jnp.sum(e, axis=0, keepdims=True)


def _merge_out(oa, ob, wa, wb, proj, gate_col, b_gate, wo, x, norm_g, wr_t, tm=256):
    m, d = x.shape
    ne = wr_t.shape[0]
    tm = _tile(m, tm)
    g0b = gate_col // d
    resident = lambda a: pl.BlockSpec(a.shape, lambda i: (0,) * a.ndim, pipeline_mode=pl.Buffered(1))
    row = lambda a: pl.BlockSpec((tm, a.shape[1]), lambda i: (i, 0))
    return pl.pallas_call(
        _merge_out_kernel,
        grid=(m // tm,),
        in_specs=[row(oa), row(ob), resident(wa), resident(wb),
                  pl.BlockSpec((tm, d), lambda i: (i, g0b)),
                  pl.BlockSpec((tm, d), lambda i: (i, g0b + 1)),
                  resident(b_gate), resident(wo), row(x), resident(norm_g), resident(wr_t)],
        out_specs=[pl.BlockSpec((tm, d), lambda i: (i, 0)), pl.BlockSpec((tm, d), lambda i: (i, 0)),
                   pl.BlockSpec((ne, tm), lambda i: (0, i))],
        out_shape=[jax.ShapeDtypeStruct((m, d), F32), jax.ShapeDtypeStruct((m, d), BF16),
                   jax.ShapeDtypeStruct((ne, m), F32)],
        compiler_params=_cparams("parallel"),
        name="merge_out",
    )(oa, ob, wa, wb, proj, proj, b_gate, wo, x, norm_g, wr_t)


def _select_kernel(aff_ref, rank_ref, cum_ref, *, cap):
    ne, n = aff_ref.shape
    nch = n // LANES

    def count(mask):
        return jnp.sum(jnp.where(mask, 1.0, 0.0), axis=1, keepdims=True)

    def bisect(i, thr):
        cand = thr | jnp.left_shift(jnp.int32(1), 30 - i)
        bits = pltpu.bitcast(aff_ref[...], jnp.int32)
        return jnp.where(count(bits >= cand) >= cap, cand, thr)

    thr = lax.fori_loop(0, 31, bisect, jnp.zeros((ne, 1), jnp.int32))
    need = cap - count(pltpu.bitcast(aff_ref[...], jnp.int32) > thr)
    tri = jnp.where(lax.broadcasted_iota(jnp.int32, (LANES, LANES), 0)
                    <= lax.broadcasted_iota(jnp.int32, (LANES, LANES), 1), 1.0, 0.0).astype(BF16)

    def chunk(j, carry):
        c_eq, c_sel = carry
        off = pl.multiple_of(j * LANES, LANES)
        bits = pltpu.bitcast(aff_ref[:, pl.ds(off, LANES)], jnp.int32)
        eq = jnp.where(bits == thr, 1.0, 0.0)
        eq_incl = _dot(eq.astype(BF16), tri)
        take = jnp.where(c_eq + eq_incl - eq < need, eq, 0.0)
        sel = jnp.where(bits > thr, 1.0, take)
        sel_incl = _dot(sel.astype(BF16), tri)
        rank = c_sel + sel_incl - sel
        rank_ref[:, pl.ds(off, LANES)] = jnp.where(sel > 0.0, rank, -1.0).astype(jnp.int32)
        cum_ref[j] = jnp.broadcast_to(c_sel, (ne, LANES)).astype(jnp.int32)
        return c_eq + eq_incl[:, LANES - 1:], c_sel + sel_incl[:, LANES - 1:]

    zero = jnp.zeros((ne, 1), F32)
    lax.fori_loop(0, nch, chunk, (zero, zero))


def _select(aff_t, cap):
    ne, n = aff_t.shape
    nch = n // LANES
    return pl.pallas_call(
        functools.partial(_select_kernel, cap=cap),
        grid=(1,),
        in_specs=[pl.BlockSpec((ne, n), lambda i: (0, 0))],
        out_specs=[pl.BlockSpec((ne, n), lambda i: (0, 0)), pl.BlockSpec((nch, ne, LANES), lambda i: (0, 0, 0))],
        out_shape=[jax.ShapeDtypeStruct((ne, n), jnp.int32), jax.ShapeDtypeStruct((nch, ne, LANES), jnp.int32)],
        compiler_params=_cparams("arbitrary"),
        name="select",
    )(aff_t)


SUB = 128
ROW_TILE = 16
DISPATCH_GROUP = 4
COMBINE_GROUP = 8


def _block_starts(cum, cap, tb):
    r = cum[::tb // LANES, :, 0]
    r = jnp.concatenate([r, jnp.full((1, r.shape[1]), cap, jnp.int32)], axis=0)
    return r.T.reshape(-1)


def _dispatch_kernel(r_ref, h_ref, rank_ref, xs_ref, stage, xstage, carry, sems, xsem, *, nb, cap_pad):
    b = pl.program_id(0)
    ne, tb = rank_ref.shape
    sub_iota = lax.broadcasted_iota(jnp.int32, (SUB, tb), 0)

    def copy(e, row):
        return pltpu.make_async_copy(stage.at[e], xs_ref.at[pl.ds(row, SUB)], sems.at[e])

    def one_hot(e, base):
        return jnp.where(sub_iota == rank_ref[pl.ds(e, 1), :] - base, 1.0, 0.0).astype(BF16)

    def run(e):
        r1 = r_ref[e * (nb + 1) + b + 1]
        a = (r_ref[e * (nb + 1) + b] // ROW_TILE) * ROW_TILE
        return a, (r1 // ROW_TILE) * ROW_TILE, (r1 - a) // SUB + 1

    @pl.when(b == 0)
    def _():
        carry[...] = jnp.zeros_like(carry)
        for e in range(ne):
            stage[e] = jnp.zeros(stage.shape[1:], BF16)
            pad = copy(e, (e + 1) * cap_pad - SUB)
            pad.start()
            pad.wait()

    @pl.when(b > 0)
    def _():
        for e in range(ne):
            copy(e, 0).wait()

    for g in range(ne // DISPATCH_GROUP):
        experts = range(g * DISPATCH_GROUP, (g + 1) * DISPATCH_GROUP)
        runs = [run(e) for e in experts]
        q = jnp.concatenate([one_hot(e, a) for e, (a, _, _) in zip(experts, runs)], axis=0)
        rows = _dot(q, h_ref[...])
        for k, (e, (a, a_next, nsub)) in enumerate(zip(experts, runs)):
            part = rows[k * SUB:(k + 1) * SUB]
            stage[e] = part.astype(BF16)
            stage[e, pl.ds(0, ROW_TILE), :] = (part[:ROW_TILE] + carry[e]).astype(BF16)

            @pl.when(nsub == 1)
            def _():
                carry[e] = stage[e, pl.ds(pl.multiple_of(a_next - a, ROW_TILE), ROW_TILE), :].astype(F32)

            copy(e, pl.multiple_of(e * cap_pad + a, ROW_TILE)).start()

    def longer(e, _):
        a, a_next, nsub = run(e)

        def sub(s, _):
            base = a + s * SUB
            xstage[...] = _dot(one_hot(e, base), h_ref[...]).astype(BF16)

            @pl.when(s == nsub - 1)
            def _():
                carry[e] = xstage[pl.ds(pl.multiple_of(a_next - base, ROW_TILE), ROW_TILE), :].astype(F32)

            cp = pltpu.make_async_copy(xstage, xs_ref.at[pl.ds(pl.multiple_of(e * cap_pad + base, ROW_TILE), SUB)], xsem)
            cp.start()
            cp.wait()
            return 0

        lax.fori_loop(1, nsub, sub, 0)
        return 0

    lax.fori_loop(0, ne, longer, 0)

    @pl.when(b == nb - 1)
    def _():
        for e in range(ne):
            copy(e, 0).wait()


def _dispatch(r, h2, rank, cap, tb):
    n, d = h2.shape
    ne = rank.shape[0]
    nb = n // tb
    cap_pad = cap + SUB
    return pl.pallas_call(
        functools.partial(_dispatch_kernel, nb=nb, cap_pad=cap_pad),
        grid_spec=pltpu.PrefetchScalarGridSpec(
            num_scalar_prefetch=1,
            grid=(nb,),
            in_specs=[pl.BlockSpec((tb, d), lambda i, r: (i, 0)), pl.BlockSpec((ne, tb), lambda i, r: (0, i))],
            out_specs=pl.BlockSpec(memory_space=pl.ANY),
            scratch_shapes=[pltpu.VMEM((ne, SUB, d), BF16), pltpu.VMEM((SUB, d), BF16),
                            pltpu.VMEM((ne, ROW_TILE, d), F32),
                            pltpu.SemaphoreType.DMA((ne,)), pltpu.SemaphoreType.DMA(())],
        ),
        out_shape=jax.ShapeDtypeStruct((ne * cap_pad, d), BF16),
        compiler_params=_cparams("arbitrary"),
        name="dispatch",
    )(r, h2, rank)


def _ffn_kernel(x_ref, wg_ref, wu_ref, wd_ref, o_ref, acc_ref, *, chains):
    f = pl.program_id(2)
    rows = x_ref.shape[1] // chains

    @pl.when(f == 0)
    def _():
        acc_ref[...] = jnp.zeros_like(acc_ref)

    wg, wu, wd = wg_ref[0].astype(BF16), wu_ref[0].astype(BF16), wd_ref[0].astype(BF16)
    for c in range(chains):
        sl = slice(c * rows, (c + 1) * rows)
        x = x_ref[0, sl, :]
        hg = _dot(x, wg)
        hu = _dot(x, wu)
        hid = (hg * _sigmoid(hg) * hu).astype(BF16)
        acc_ref[sl, :] += _dot(hid, wd)

    @pl.when(f == pl.num_programs(2) - 1)
    def _():
        o_ref[...] = acc_ref[...].astype(o_ref.dtype)


def _ffn(xs, layer, wg, wu, wd, cap, t=1024, tf=512, chains=1):
    _, ne, d, fdim = wg.shape
    t, tf = _tile(cap, t), _tile(fdim, tf)
    nt = cap // t
    xs3 = xs.reshape(ne, cap + SUB, d)
    return pl.pallas_call(
        functools.partial(_ffn_kernel, chains=chains),
        grid=(ne, nt, fdim // tf),
        in_specs=[pl.BlockSpec((1, t, d), lambda e, i, f: (e, i, 0)),
                  pl.BlockSpec((None, 1, d, tf), lambda e, i, f: (layer, e, 0, f)),
                  pl.BlockSpec((None, 1, d, tf), lambda e, i, f: (layer, e, 0, f)),
                  pl.BlockSpec((None, 1, tf, d), lambda e, i, f: (layer, e, f, 0))],
        out_specs=pl.BlockSpec((t, d), lambda e, i, f: (e * nt + i, 0)),
        out_shape=jax.ShapeDtypeStruct((ne * cap, d), BF16),
        scratch_shapes=[pltpu.VMEM((t, d), F32)],
        compiler_params=_cparams("parallel", "parallel", "arbitrary"),
        name="expert_ffn",
    )(xs3, wg, wu, wd)


def _combine_kernel(r_ref, x_ref, rank_ref, aff_ref, y_ref, o_ref, ybuf, xbuf, sems, xsem, *, nb, cap):
    b = pl.program_id(0)
    ne, tb = rank_ref.shape
    total = ne * cap
    sub_iota = lax.broadcasted_iota(jnp.int32, (SUB, tb), 0)

    def first_row(bb, e):
        return ((e * cap + r_ref[e * (nb + 1) + bb]) // ROW_TILE) * ROW_TILE

    def fetch(bb, e, slot):
        row = pl.multiple_of(jnp.minimum(first_row(bb, e), total - SUB), ROW_TILE)
        return pltpu.make_async_copy(y_ref.at[pl.ds(row, SUB)], ybuf.at[slot, e // COMBINE_GROUP, pl.ds((e % COMBINE_GROUP) * SUB, SUB)],
                                     sems.at[slot, e])

    @pl.when(b == 0)
    def _():
        for e in range(ne):
            fetch(0, e, 0).start()

    @pl.when(b + 1 < nb)
    def _():
        for e in range(ne):
            fetch(b + 1, e, (b + 1) % 2).start()

    slot = b % 2
    o_ref[...] = x_ref[...]

    def weights(e, start, lower):
        rank_row = rank_ref[pl.ds(e, 1), :]
        row = rank_row + e * cap
        gate = jnp.where(rank_row >= 0, jnp.where(row >= lower, aff_ref[pl.ds(e, 1), :], 0.0), 0.0)
        return jnp.where(sub_iota == row - start, gate, 0.0).astype(BF16)

    def scatter(w, rows):
        return lax.dot_general(w, rows, (((0,), (0,)), ((), ())), preferred_element_type=F32)

    for g in range(ne // COMBINE_GROUP):
        ws = []
        for e in range(g * COMBINE_GROUP, (g + 1) * COMBINE_GROUP):
            fetch(b, e, slot).wait()
            a = first_row(b, e)
            ws.append(weights(e, jnp.minimum(a, total - SUB), a))
        o_ref[...] += scatter(jnp.concatenate(ws, axis=0), ybuf[slot, g])

    def extra(e, _):
        a = first_row(b, e)
        r1 = e * cap + r_ref[e * (nb + 1) + b + 1]

        def sub(s, _):
            lower = a + s * SUB
            start = pl.multiple_of(jnp.minimum(lower, total - SUB), ROW_TILE)
            cp = pltpu.make_async_copy(y_ref.at[pl.ds(start, SUB)], xbuf, xsem)
            cp.start()
            cp.wait()
            o_ref[...] += scatter(weights(e, start, lower), xbuf[...])
            return 0

        lax.fori_loop(1, (r1 - a + SUB - 1) // SUB, sub, 0)
        return 0

    lax.fori_loop(0, ne, extra, 0)


def _combine(r, x, rank, aff_t, y, cap, tb):
    n, d = x.shape
    ne = rank.shape[0]
    nb = n // tb
    return pl.pallas_call(
        functools.partial(_combine_kernel, nb=nb, cap=cap),
        grid_spec=pltpu.PrefetchScalarGridSpec(
            num_scalar_prefetch=1,
            grid=(nb,),
            in_specs=[pl.BlockSpec((tb, d), lambda i, r: (i, 0)),
                      pl.BlockSpec((ne, tb), lambda i, r: (0, i)),
                      pl.BlockSpec((ne, tb), lambda i, r: (0, i)),
                      pl.BlockSpec(memory_space=pl.ANY)],
            out_specs=pl.BlockSpec((tb, d), lambda i, r: (i, 0)),
            scratch_shapes=[pltpu.VMEM((2, ne // COMBINE_GROUP, COMBINE_GROUP * SUB, d), BF16),
                            pltpu.VMEM((SUB, d), BF16),
                            pltpu.SemaphoreType.DMA((2, ne)), pltpu.SemaphoreType.DMA(())],
        ),
        out_shape=jax.ShapeDtypeStruct((n, d), F32),
        compiler_params=_cparams("arbitrary"),
        name="combine",
    )(r, x, rank, aff_t, y)


def _moe(x, h2, aff_t, layer, wg, wu, wd, tb=512):
    n = x.shape[0]
    ne = aff_t.shape[0]
    cap = EC_CAPACITY_FACTOR * n // ne
    tb = _tile(n, tb)
    rank, cum = _select(aff_t, cap)
    r = _block_starts(cum, cap, tb)
    xs = _dispatch(r, h2, rank, cap, tb)
    y = _ffn(xs, layer, wg, wu, wd, cap)
    return _combine(r, x, rank, aff_t, y, cap, tb)


def _rope_tables(seq):
    def tables(dim, reps):
        inv = 1.0 / (ROPE_THETA ** (jnp.arange(0, dim, 2, dtype=F32) / dim))
        ang = jnp.arange(seq, dtype=F32)[:, None] * inv[None, :]
        c, s = jnp.cos(ang), jnp.sin(ang)
        z = jnp.zeros_like(c)
        pad = jnp.zeros((seq, LANES - reps * dim), F32)
        cos = jnp.concatenate([c, c] * reps + [pad], axis=-1)
        s1 = jnp.concatenate([-s, z] * reps + [pad], axis=-1)
        s2 = jnp.concatenate([z, s] * reps + [pad], axis=-1)
        return cos, s1, s2

    return tables(MLA_ROPE, 1), tables(DIFF_QK, LANES // DIFF_QK)


def _prep_layer(l, p):
    d = p["w_in"].shape[1]
    w_in = p["w_in"][l]
    o_cq, o_ckv, o_kpe = 0, Q_LORA, Q_LORA + KV_LORA
    o_qd = o_kpe + MLA_ROPE
    dw = DIFF_HEADS * 2 * DIFF_QK
    o_kd, o_vd = o_qd + dw, o_qd + 2 * dw
    o_g = o_vd + DIFF_HEADS * DIFF_V
    w_in_p = jnp.concatenate(
        [w_in[:, o_g:], w_in[:, o_qd:o_g], w_in[:, :o_qd], jnp.zeros((d, MLA_SEG - o_qd), F32)], axis=1).astype(BF16)
    cols = dict(gate=0, qd=2 * d, kd=2 * d + dw, vd=2 * d + 2 * dw, mla=2 * d + 3 * dw)
    wq = p["w_q_up"][l].reshape(Q_LORA, MLA_HEADS, MLA_QK)
    wq = jnp.pad(wq, ((0, 0), (0, 0), (0, HEAD_PAD - MLA_QK))).reshape(Q_LORA, MLA_HEADS * HEAD_PAD).astype(BF16)
    wkv = p["w_kv_up"][l].reshape(KV_LORA, MLA_HEADS, MLA_NOPE + MLA_V)
    wkv = jnp.concatenate([wkv[:, :, :MLA_NOPE].reshape(KV_LORA, -1), wkv[:, :, MLA_NOPE:].reshape(KV_LORA, -1)],
                          axis=1).astype(BF16)
    row = lambda a: a.reshape(1, -1).astype(F32)
    pad_g = lambda a: jnp.pad(a, (0, HEAD_PAD - MLA_QK)).reshape(1, -1)
    diff_g = lambda a: jnp.stack([jnp.tile(a, LANES // DIFF_QK),
                                  jnp.tile(jnp.roll(a, DIFF_QK // 2), LANES // DIFF_QK)]).astype(F32)
    lam_init = 0.8 - 0.6 * math.exp(-0.3 * l)
    lam = (jnp.exp(jnp.sum(p["lambda_q1"][l].astype(F32) * p["lambda_k1"][l].astype(F32)))
           - jnp.exp(jnp.sum(p["lambda_q2"][l].astype(F32) * p["lambda_k2"][l].astype(F32))) + lam_init)
    return dict(
        cols=cols, w_in=w_in_p, norm_attn_g=row(p["norm_attn_g"][l]), b_gate=row(p["b_gate"][l]),
        cq_g=row(p["mla_cq_g"][l]), ckv_g=row(p["mla_ckv_g"][l]), wq=wq, wkv=wkv,
        q_g=pad_g(p["mla_q_g"][l]), k_g=pad_g(p["mla_k_g"][l]),
        dq_g=diff_g(p["diff_q_g"][l]), dk_g=diff_g(p["diff_k_g"][l]),
        lam=lam.reshape(1).astype(F32), out_scale=1.0 - lam_init, sub_g=row(p["diff_sub_g"][l]),
        wa=p["w_a_out"][l].astype(BF16), wb=p["w_b_out"][l].astype(BF16), wo=p["w_o"][l].astype(BF16),
        norm_ffn_g=row(p["norm_ffn_g"][l]), wr_t=p["w_router"][l].T.astype(F32),
        layer=l, wg=p["w_e_gate"], wu=p["w_e_up"], wd=p["w_e_down"],
    )


def _layer(x, lp, ropes, batch, seq):
    m, d = x.shape
    rope_a, rope_b = ropes
    cols = lp["cols"]
    proj = _norm_mm(x, lp["norm_attn_g"], lp["w_in"], BF16)
    q_a, k_a, v_a = _mla_prep(proj, cols["mla"], lp["cq_g"], lp["ckv_g"], lp["wq"], lp["wkv"], lp["q_g"], lp["k_g"],
                              rope_a, seq)
    o_a = _mla_attn(q_a, k_a, v_a, batch, seq)
    q_d, k_d, v_d = _diff_prep(proj, cols["qd"], cols["kd"], cols["vd"], lp["dq_g"], lp["dk_g"], rope_b, seq)
    o_b = _diff_attn(lp["lam"], q_d, k_d, v_d, lp["sub_g"], lp["out_scale"], batch, seq)
    x, h2, aff_t = _merge_out(o_a, o_b, lp["wa"], lp["wb"], proj, cols["gate"], lp["b_gate"], lp["wo"], x,
                              lp["norm_ffn_g"], lp["wr_t"])
    return _moe(x, h2, aff_t, lp["layer"], lp["wg"], lp["wu"], lp["wd"])


def kernel(x_prompt, x_sample, norm_attn_g, w_in, b_gate, mla_cq_g, w_q_up, mla_ckv_g, w_kv_up, mla_q_g, mla_k_g,
           w_a_out, diff_q_g, diff_k_g, lambda_q1, lambda_k1, lambda_q2, lambda_k2, diff_sub_g, w_b_out, w_o,
           norm_ffn_g, w_router, w_e_gate, w_e_up, w_e_down):
    p = dict(norm_attn_g=norm_attn_g, w_in=w_in, b_gate=b_gate, mla_cq_g=mla_cq_g, w_q_up=w_q_up,
             mla_ckv_g=mla_ckv_g, w_kv_up=w_kv_up, mla_q_g=mla_q_g, mla_k_g=mla_k_g, w_a_out=w_a_out,
             diff_q_g=diff_q_g, diff_k_g=diff_k_g, lambda_q1=lambda_q1, lambda_k1=lambda_k1, lambda_q2=lambda_q2,
             lambda_k2=lambda_k2, diff_sub_g=diff_sub_g, w_b_out=w_b_out, w_o=w_o, norm_ffn_g=norm_ffn_g,
             w_router=w_router, w_e_gate=w_e_gate, w_e_up=w_e_up, w_e_down=w_e_down)
    depth = w_in.shape[0]
    groups = [x_prompt, x_sample]
    shapes = [g.shape for g in groups]
    xs = [g.reshape(-1, g.shape[-1]) for g in groups]
    ropes = [_rope_tables(s[1]) for s in shapes]
    for l in range(depth):
        lp = _prep_layer(l, p)
        xs = [_layer(x, lp, r, s[0], s[1]) for x, r, s in zip(xs, ropes, shapes)]
    return tuple(x.reshape(s) for x, s in zip(xs, shapes))
```

```python
import functools
import math

import jax
import jax.numpy as jnp
from jax import lax
from jax.experimental import pallas as pl
from jax.experimental.pallas import tpu as pltpu

F32 = jnp.float32
BF16 = jnp.bfloat16

MLA_HEADS = 8
MLA_NOPE = 128
MLA_ROPE = 64
MLA_QK = MLA_NOPE + MLA_ROPE
MLA_V = 128
Q_LORA = 768
KV_LORA = 512
DIFF_HEADS = 8
DIFF_QK = 64
DIFF_V = 128
N_EXPERTS = 16
EC_CAPACITY_FACTOR = 2
ROPE_THETA = 10000.0
EPS = 1e-6

LOG2E = 1.4426950408889634
LANES = 128
HEAD_PAD = 2 * LANES
MLA_SEG = 1536
VMEM_LIMIT = 56 * 1024 * 1024
CHAIN_ROWS = 256
MLA_SCORE_ELEMS = 2048 * 4096
DIFF_SCORE_ELEMS = 2048 * 2048


def _cparams(*sem):
    return pltpu.CompilerParams(dimension_semantics=sem, vmem_limit_bytes=VMEM_LIMIT)


def _rms(x, g):
    ms = jnp.mean(x * x, axis=-1, keepdims=True)
    return x * lax.rsqrt(ms + EPS) * g


def _dot(a, b):
    return jnp.dot(a, b, preferred_element_type=F32)


def _dot_nt(a, b, precision=None):
    return lax.dot_general(a, b, (((1,), (1,)), ((), ())), preferred_element_type=F32, precision=precision)


def _sigmoid(z):
    return 1.0 / (1.0 + jnp.exp(-z))


def _tile(n, t):
    t = min(n, t)
    assert n % t == 0, (n, t)
    return t


def _norm_mm_kernel(x_ref, g_ref, w_ref, o_ref, xn_ref):
    @pl.when(pl.program_id(1) == 0)
    def _():
        xn_ref[...] = _rms(x_ref[...].astype(F32), g_ref[...]).astype(BF16)

    o_ref[...] = _dot(xn_ref[...], w_ref[...]).astype(o_ref.dtype)


def _norm_mm(x, g, w, out_dtype, tm=1024, tn=512):
    m, k = x.shape
    n = w.shape[1]
    tm, tn = _tile(m, tm), _tile(n, tn)
    return pl.pallas_call(
        _norm_mm_kernel,
        grid=(m // tm, n // tn),
        in_specs=[
            pl.BlockSpec((tm, k), lambda i, j: (i, 0)),
            pl.BlockSpec((1, k), lambda i, j: (0, 0)),
            pl.BlockSpec((k, tn), lambda i, j: (0, j)),
        ],
        out_specs=pl.BlockSpec((tm, tn), lambda i, j: (i, j)),
        out_shape=jax.ShapeDtypeStruct((m, n), out_dtype),
        scratch_shapes=[pltpu.VMEM((tm, k), BF16)],
        compiler_params=_cparams("parallel", "arbitrary"),
        name="in_proj",
    )(x, g, w)


def _rope128(t, cos, s1, s2):
    return t * cos + pltpu.roll(t, 96, 1) * s1 + pltpu.roll(t, 32, 1) * s2


def _mla_prep_kernel(p0_ref, p1_ref, p2_ref, cqg_ref, ckvg_ref, wq_ref, wkv_ref, qg_ref, kg_ref,
                     cos_ref, s1_ref, s2_ref, q_out, k_out, v_out):
    p = jnp.concatenate([p0_ref[...], p1_ref[...], p2_ref[...]], axis=1)
    cq = p[:, :Q_LORA].astype(F32)
    ckv = p[:, Q_LORA:Q_LORA + KV_LORA].astype(F32)
    kpe = p[:, Q_LORA + KV_LORA:Q_LORA + KV_LORA + LANES].astype(F32)
    q = _dot(_rms(cq, cqg_ref[...]).astype(BF16), wq_ref[...])
    kv = _dot(_rms(ckv, ckvg_ref[...]).astype(BF16), wkv_ref[...])
    cos, s1, s2 = cos_ref[...], s1_ref[...], s2_ref[...]
    qg, kg = qg_ref[...], kg_ref[...]
    scale = LOG2E / math.sqrt(MLA_QK)
    kpe_sq = jnp.sum(kpe * kpe, axis=-1, keepdims=True)
    kpe_r = _rope128(kpe * kg[:, LANES:], cos, s1, s2)
    nk = MLA_HEADS * MLA_NOPE
    for h in range(MLA_HEADS):
        qh = q[:, h * HEAD_PAD:(h + 1) * HEAD_PAD]
        r = lax.rsqrt(jnp.sum(qh * qh, axis=-1, keepdims=True) / MLA_QK + EPS) * scale
        q_out[:, h * HEAD_PAD:h * HEAD_PAD + LANES] = (qh[:, :LANES] * r * qg[:, :LANES]).astype(BF16)
        q_hi = _rope128(qh[:, LANES:] * qg[:, LANES:], cos, s1, s2) * r
        q_out[:, h * HEAD_PAD + LANES:(h + 1) * HEAD_PAD] = q_hi.astype(BF16)
        kn = kv[:, h * MLA_NOPE:(h + 1) * MLA_NOPE]
        rk = lax.rsqrt((jnp.sum(kn * kn, axis=-1, keepdims=True) + kpe_sq) / MLA_QK + EPS)
        k_out[:, h * HEAD_PAD:h * HEAD_PAD + LANES] = (kn * rk * kg[:, :LANES]).astype(BF16)
        k_out[:, h * HEAD_PAD + LANES:(h + 1) * HEAD_PAD] = (kpe_r * rk).astype(BF16)
        v_out[:, 2 * h * MLA_V:(2 * h + 1) * MLA_V] = kv[:, nk + h * MLA_V:nk + (h + 1) * MLA_V].astype(BF16)
        v_out[:, (2 * h + 1) * MLA_V:(2 * h + 2) * MLA_V] = jnp.ones((kv.shape[0], MLA_V), BF16)


def _mla_prep(proj, mla_col, cqg, ckvg, wq, wkv, qg, kg, rope, seq, tm=512):
    m = proj.shape[0]
    tm = _tile(seq, tm)
    nsb = seq // tm
    pw = MLA_SEG // 3
    pspec = lambda k: pl.BlockSpec((tm, pw), lambda i: (i, mla_col // pw + k))
    full = lambda a: pl.BlockSpec(a.shape, lambda i: (0,) * a.ndim)
    rspec = pl.BlockSpec((tm, LANES), lambda i: (i % nsb, 0))
    hq = MLA_HEADS * HEAD_PAD
    return pl.pallas_call(
        _mla_prep_kernel,
        grid=(m // tm,),
        in_specs=[pspec(0), pspec(1), pspec(2), full(cqg), full(ckvg), full(wq), full(wkv),
                  full(qg), full(kg), rspec, rspec, rspec],
        out_specs=[pl.BlockSpec((tm, hq), lambda i: (i, 0)), pl.BlockSpec((tm, hq), lambda i: (i, 0)),
                   pl.BlockSpec((tm, 2 * MLA_HEADS * MLA_V), lambda i: (i, 0))],
        out_shape=[jax.ShapeDtypeStruct((m, hq), BF16), jax.ShapeDtypeStruct((m, hq), BF16),
                   jax.ShapeDtypeStruct((m, 2 * MLA_HEADS * MLA_V), BF16)],
        compiler_params=_cparams("parallel"),
        name="mla_prep",
    )(proj, proj, proj, cqg, ckvg, wq, wkv, qg, kg, *rope)


def _diff_prep_kernel(q_ref, k_ref, v_ref, qg_ref, kg_ref, cos_ref, s1_ref, s2_ref, q_out, k_out, v_out):
    cos, sin = cos_ref[...], s2_ref[...] - s1_ref[...]
    nh = q_ref.shape[1] // LANES
    i = lax.broadcasted_iota(jnp.int32, (LANES, LANES), 0)
    j = lax.broadcasted_iota(jnp.int32, (LANES, LANES), 1)
    half = DIFF_QK // 2
    first = (j % DIFF_QK) < half
    same = jnp.where(i // DIFF_QK == j // DIFF_QK, 1.0, 0.0).astype(BF16)
    perm = jnp.where(i == j + half, jnp.where(first, -1.0, 0.0),
                     jnp.where(i == j - half, jnp.where(first, 0.0, 1.0), 0.0)).astype(BF16)

    def prep(ref, g_ref, out, scale):
        t1 = g_ref[0:1, :] * cos
        t2 = g_ref[1:2, :] * sin
        for h in range(nh):
            xb = ref[:, h * LANES:(h + 1) * LANES]
            x = xb.astype(F32)
            ssq = _dot((x * x).astype(BF16), same)
            r = lax.rsqrt(ssq / DIFF_QK + EPS) * scale
            out[:, h * LANES:(h + 1) * LANES] = ((x * t1 + _dot(xb, perm) * t2) * r).astype(BF16)

    prep(q_ref, qg_ref, q_out, LOG2E / math.sqrt(DIFF_QK))
    prep(k_ref, kg_ref, k_out, 1.0)
    for h in range(nh):
        v_out[:, 2 * h * LANES:(2 * h + 1) * LANES] = v_ref[:, h * LANES:(h + 1) * LANES]
        v_out[:, (2 * h + 1) * LANES:(2 * h + 2) * LANES] = jnp.ones((v_ref.shape[0], LANES), BF16)


def _diff_prep(proj, q_col, k_col, v_col, qg, kg, rope, seq, tm=512, tn=512):
    m = proj.shape[0]
    width = DIFF_HEADS * 2 * DIFF_QK
    tm = _tile(seq, tm)
    nsb = seq // tm
    nj = width // tn
    full = lambda a: pl.BlockSpec(a.shape, lambda i, j: (0,) * a.ndim)
    rspec = pl.BlockSpec((tm, LANES), lambda i, j: (i % nsb, 0))
    ospec = pl.BlockSpec((tm, tn), lambda i, j: (i, j))
    return pl.pallas_call(
        _diff_prep_kernel,
        grid=(m // tm, nj),
        in_specs=[pl.BlockSpec((tm, tn), lambda i, j: (i, q_col // tn + j)),
                  pl.BlockSpec((tm, tn), lambda i, j: (i, k_col // tn + j)),
                  pl.BlockSpec((tm, tn), lambda i, j: (i, v_col // tn + j)),
                  full(qg), full(kg), rspec, rspec, rspec],
        out_specs=[ospec, ospec, pl.BlockSpec((tm, 2 * tn), lambda i, j: (i, j))],
        out_shape=[jax.ShapeDtypeStruct((m, width), BF16)] * 2 + [jax.ShapeDtypeStruct((m, 2 * width), BF16)],
        compiler_params=_cparams("parallel", "parallel"),
        name="diff_prep",
    )(proj, proj, proj, qg, kg, *rope)


def _softmax_pv(q, k, v1):
    s = _dot_nt(q, k)
    p = jnp.exp2((s - jnp.max(s, axis=-1, keepdims=True)).astype(BF16))
    o = _dot(p, v1)
    return o[:, :LANES] / o[:, LANES:]


def _mla_attn_kernel(q_ref, k_ref, v_ref, o_ref, *, chains):
    rows = q_ref.shape[0] // chains
    k, v1 = k_ref[...], v_ref[...]
    for c in range(chains):
        sl = slice(c * rows, (c + 1) * rows)
        o_ref[sl, :] = _softmax_pv(q_ref[sl, :], k, v1).astype(o_ref.dtype)


def _attn_rows(seq, score_elems):
    return max(CHAIN_ROWS, min(seq, score_elems // seq))


def _mla_attn(q, k, v1, batch, seq):
    m = q.shape[0]
    tq = _attn_rows(seq, MLA_SCORE_ELEMS)
    chains = tq // CHAIN_ROWS
    nq = seq // tq
    return pl.pallas_call(
        functools.partial(_mla_attn_kernel, chains=chains),
        grid=(batch, MLA_HEADS, nq),
        in_specs=[pl.BlockSpec((tq, HEAD_PAD), lambda b, h, i: (b * nq + i, h)),
                  pl.BlockSpec((seq, HEAD_PAD), lambda b, h, i: (b, h)),
                  pl.BlockSpec((seq, 2 * MLA_V), lambda b, h, i: (b, h))],
        out_specs=pl.BlockSpec((tq, MLA_V), lambda b, h, i: (b * nq + i, h)),
        out_shape=jax.ShapeDtypeStruct((m, MLA_HEADS * MLA_V), BF16),
        compiler_params=_cparams("parallel", "parallel", "arbitrary"),
        name="mla_attn",
    )(q, k, v1)


def _diff_attn_kernel(lam_ref, q_ref, k_ref, v_ref, g_ref, o_ref, *, out_scale, chains):
    rows = q_ref.shape[0] // chains
    k, v1 = k_ref[...], v_ref[...]
    lo = lax.broadcasted_iota(jnp.int32, (1, LANES), 1) < DIFF_QK
    for c in range(chains):
        sl = slice(c * rows, (c + 1) * rows)
        q = q_ref[sl, :]
        zero = jnp.zeros_like(q)
        o1 = _softmax_pv(jnp.where(lo, q, zero), k, v1)
        o2 = _softmax_pv(jnp.where(lo, zero, q), k, v1)
        o = o1 - lam_ref[0] * o2
        o_ref[sl, :] = (_rms(o, g_ref[...]) * out_scale).astype(o_ref.dtype)


def _diff_attn(lam, q, k, v1, sub_g, out_scale, batch, seq):
    m = q.shape[0]
    tq = _attn_rows(seq, DIFF_SCORE_ELEMS)
    chains = tq // CHAIN_ROWS
    nq = seq // tq
    return pl.pallas_call(
        functools.partial(_diff_attn_kernel, out_scale=out_scale, chains=chains),
        grid=(batch, DIFF_HEADS, nq),
        in_specs=[pl.BlockSpec(memory_space=pltpu.SMEM),
                  pl.BlockSpec((tq, LANES), lambda b, h, i: (b * nq + i, h)),
                  pl.BlockSpec((seq, LANES), lambda b, h, i: (b, h)),
                  pl.BlockSpec((seq, 2 * DIFF_V), lambda b, h, i: (b, h)),
                  pl.BlockSpec((1, DIFF_V), lambda b, h, i: (0, 0))],
        out_specs=pl.BlockSpec((tq, DIFF_V), lambda b, h, i: (b * nq + i, h)),
        out_shape=jax.ShapeDtypeStruct((m, DIFF_HEADS * DIFF_V), BF16),
        compiler_params=_cparams("parallel", "parallel", "arbitrary"),
        name="diff_attn",
    )(lam, q, k, v1, sub_g)


def _merge_out_kernel(oa_ref, ob_ref, wa_ref, wb_ref, g0_ref, g1_ref, b_ref, wo_ref, x_ref, ng_ref, wr_ref,
                      o_ref, h_ref, aff_ref):
    d = wa_ref.shape[1]
    a = _dot(oa_ref[...], wa_ref[...])
    b = _dot(ob_ref[...], wb_ref[...])
    g0 = _sigmoid(g0_ref[...].astype(F32) + b_ref[:, :d])
    g1 = _sigmoid(g1_ref[...].astype(F32) + b_ref[:, d:])
    merged = (g0 * a + g1 * b).astype(BF16)
    x = x_ref[...] + _dot(merged, wo_ref[...])
    o_ref[...] = x
    xn = _rms(x, ng_ref[...])
    h_ref[...] = xn.astype(BF16)
    wr = wr_ref[...]
    xh, wh = xn.astype(BF16), wr.astype(BF16)
    xl, wl = (xn - xh.astype(F32)).astype(BF16), (wr - wh.astype(F32)).astype(BF16)
    lg = _dot(xh, wh) + (_dot(xh, wl) + _dot(xl, wh))
    lg = lg.T[:aff_ref.shape[0]]
    e = jnp.exp(lg - jnp.max(lg, axis=0, keepdims=True))
    aff_ref[...] = e / jnp.sum(e, axis=0, keepdims=True)


def _merge_out(oa, ob, wa, wb, proj, gate_col, b_gate, wo, x, norm_g, wr, ne, tm=256):
    m, d = x.shape
    tm = _tile(m, tm)
    g0b = gate_col // d
    resident = lambda a: pl.BlockSpec(a.shape, lambda i: (0,) * a.ndim, pipeline_mode=pl.Buffered(1))
    row = lambda a: pl.BlockSpec((tm, a.shape[1]), lambda i: (i, 0))
    return pl.pallas_call(
        _merge_out_kernel,
        grid=(m // tm,),
        in_specs=[row(oa), row(ob), resident(wa), resident(wb),
                  pl.BlockSpec((tm, d), lambda i: (i, g0b)),
                  pl.BlockSpec((tm, d), lambda i: (i, g0b + 1)),
                  resident(b_gate), resident(wo), row(x), resident(norm_g), resident(wr)],
        out_specs=[pl.BlockSpec((tm, d), lambda i: (i, 0)), pl.BlockSpec((tm, d), lambda i: (i, 0)),
                   pl.BlockSpec((ne, tm), lambda i: (0, i))],
        out_shape=[jax.ShapeDtypeStruct((m, d), F32), jax.ShapeDtypeStruct((m, d), BF16),
                   jax.ShapeDtypeStruct((ne, m), F32)],
        compiler_params=_cparams("parallel"),
        name="merge_out",
    )(oa, ob, wa, wb, proj, proj, b_gate, wo, x, norm_g, wr)


def _select_kernel(aff_ref, rank_ref, cum_ref, *, cap):
    ne, n = aff_ref.shape
    nch = n // LANES

    def count(mask):
        return jnp.sum(jnp.where(mask, 1.0, 0.0), axis=1, keepdims=True)

    def bisect(i, thr):
        cand = thr | jnp.left_shift(jnp.int32(1), 30 - i)
        bits = pltpu.bitcast(aff_ref[...], jnp.int32)
        return jnp.where(count(bits >= cand) >= cap, cand, thr)

    thr = lax.fori_loop(0, 31, bisect, jnp.zeros((ne, 1), jnp.int32))
    need = cap - count(pltpu.bitcast(aff_ref[...], jnp.int32) > thr)
    tri = jnp.where(lax.broadcasted_iota(jnp.int32, (LANES, LANES), 0)
                    <= lax.broadcasted_iota(jnp.int32, (LANES, LANES), 1), 1.0, 0.0).astype(BF16)

    def chunk(j, carry):
        c_eq, c_sel = carry
        off = pl.multiple_of(j * LANES, LANES)
        bits = pltpu.bitcast(aff_ref[:, pl.ds(off, LANES)], jnp.int32)
        eq = jnp.where(bits == thr, 1.0, 0.0)
        eq_incl = _dot(eq.astype(BF16), tri)
        take = jnp.where(c_eq + eq_incl - eq < need, eq, 0.0)
        sel = jnp.where(bits > thr, 1.0, take)
        sel_incl = _dot(sel.astype(BF16), tri)
        rank = c_sel + sel_incl - sel
        rank_ref[:, pl.ds(off, LANES)] = jnp.where(sel > 0.0, rank, -1.0).astype(jnp.int32)
        cum_ref[j] = jnp.broadcast_to(c_sel, (ne, LANES)).astype(jnp.int32)
        return c_eq + eq_incl[:, LANES - 1:], c_sel + sel_incl[:, LANES - 1:]

    zero = jnp.zeros((ne, 1), F32)
    lax.fori_loop(0, nch, chunk, (zero, zero))


def _select(aff_t, cap):
    ne, n = aff_t.shape
    nch = n // LANES
    return pl.pallas_call(
        functools.partial(_select_kernel, cap=cap),
        grid=(1,),
        in_specs=[pl.BlockSpec((ne, n), lambda i: (0, 0))],
        out_specs=[pl.BlockSpec((ne, n), lambda i: (0, 0)), pl.BlockSpec((nch, ne, LANES), lambda i: (0, 0, 0))],
        out_shape=[jax.ShapeDtypeStruct((ne, n), jnp.int32), jax.ShapeDtypeStruct((nch, ne, LANES), jnp.int32)],
        compiler_params=_cparams("arbitrary"),
        name="select",
    )(aff_t)


SUB = 128
ROW_TILE = 16
DISPATCH_GROUP = 4
COMBINE_GROUP = 8


def _block_starts(cum, cap, tb):
    r = cum[::tb // LANES, :, 0]
    r = jnp.concatenate([r, jnp.full((1, r.shape[1]), cap, jnp.int32)], axis=0)
    return r.T.reshape(-1)


def _dispatch_kernel(r_ref, h_ref, rank_ref, xs_ref, stage, xstage, carry, sems, xsem, *, nb, cap_pad):
    b = pl.program_id(0)
    ne, tb = rank_ref.shape
    sub_iota = lax.broadcasted_iota(jnp.int32, (SUB, tb), 0)

    def copy(e, row):
        return pltpu.make_async_copy(stage.at[e], xs_ref.at[pl.ds(row, SUB)], sems.at[e])

    def one_hot(e, base):
        return jnp.where(sub_iota == rank_ref[pl.ds(e, 1), :] - base, 1.0, 0.0).astype(BF16)

    def run(e):
        r1 = r_ref[e * (nb + 1) + b + 1]
        a = (r_ref[e * (nb + 1) + b] // ROW_TILE) * ROW_TILE
        return a, (r1 // ROW_TILE) * ROW_TILE, (r1 - a) // SUB + 1

    @pl.when(b == 0)
    def _():
        carry[...] = jnp.zeros_like(carry)
        for e in range(ne):
            stage[e] = jnp.zeros(stage.shape[1:], BF16)
            pad = copy(e, (e + 1) * cap_pad - SUB)
            pad.start()
            pad.wait()

    @pl.when(b > 0)
    def _():
        for e in range(ne):
            copy(e, 0).wait()

    for g in range(ne // DISPATCH_GROUP):
        experts = range(g * DISPATCH_GROUP, (g + 1) * DISPATCH_GROUP)
        runs = [run(e) for e in experts]
        q = jnp.concatenate([one_hot(e, a) for e, (a, _, _) in zip(experts, runs)], axis=0)
        rows = _dot(q, h_ref[...])
        for k, (e, (a, a_next, nsub)) in enumerate(zip(experts, runs)):
            part = rows[k * SUB:(k + 1) * SUB]
            stage[e] = part.astype(BF16)
            stage[e, pl.ds(0, ROW_TILE), :] = (part[:ROW_TILE] + carry[e]).astype(BF16)

            @pl.when(nsub == 1)
            def _():
                carry[e] = stage[e, pl.ds(pl.multiple_of(a_next - a, ROW_TILE), ROW_TILE), :].astype(F32)

            copy(e, pl.multiple_of(e * cap_pad + a, ROW_TILE)).start()

    def longer(e, _):
        a, a_next, nsub = run(e)

        def sub(s, _):
            base = a + s * SUB
            xstage[...] = _dot(one_hot(e, base), h_ref[...]).astype(BF16)

            @pl.when(s == nsub - 1)
            def _():
                carry[e] = xstage[pl.ds(pl.multiple_of(a_next - base, ROW_TILE), ROW_TILE), :].astype(F32)

            cp = pltpu.make_async_copy(xstage, xs_ref.at[pl.ds(pl.multiple_of(e * cap_pad + base, ROW_TILE), SUB)], xsem)
            cp.start()
            cp.wait()
            return 0

        lax.fori_loop(1, nsub, sub, 0)
        return 0

    lax.fori_loop(0, ne, longer, 0)

    @pl.when(b == nb - 1)
    def _():
        for e in range(ne):
            copy(e, 0).wait()


def _dispatch(r, h2, rank, cap, tb):
    n, d = h2.shape
    ne = rank.shape[0]
    nb = n // tb
    cap_pad = cap + SUB
    return pl.pallas_call(
        functools.partial(_dispatch_kernel, nb=nb, cap_pad=cap_pad),
        grid_spec=pltpu.PrefetchScalarGridSpec(
            num_scalar_prefetch=1,
            grid=(nb,),
            in_specs=[pl.BlockSpec((tb, d), lambda i, r: (i, 0)), pl.BlockSpec((ne, tb), lambda i, r: (0, i))],
            out_specs=pl.BlockSpec(memory_space=pl.ANY),
            scratch_shapes=[pltpu.VMEM((ne, SUB, d), BF16), pltpu.VMEM((SUB, d), BF16),
                            pltpu.VMEM((ne, ROW_TILE, d), F32),
                            pltpu.SemaphoreType.DMA((ne,)), pltpu.SemaphoreType.DMA(())],
        ),
        out_shape=jax.ShapeDtypeStruct((ne * cap_pad, d), BF16),
        compiler_params=_cparams("arbitrary"),
        name="dispatch",
    )(r, h2, rank)


def _ffn_kernel(x_ref, wg_ref, wu_ref, wd_ref, o_ref, acc_ref, *, chains):
    f = pl.program_id(2)
    rows = x_ref.shape[1] // chains

    @pl.when(f == 0)
    def _():
        acc_ref[...] = jnp.zeros_like(acc_ref)

    wg, wu, wd = wg_ref[0].astype(BF16), wu_ref[0].astype(BF16), wd_ref[0].astype(BF16)
    for c in range(chains):
        sl = slice(c * rows, (c + 1) * rows)
        x = x_ref[0, sl, :]
        hg = _dot(x, wg)
        hu = _dot(x, wu)
        hid = (hg * _sigmoid(hg) * hu).astype(BF16)
        acc_ref[sl, :] += _dot(hid, wd)

    @pl.when(f == pl.num_programs(2) - 1)
    def _():
        o_ref[...] = acc_ref[...].astype(o_ref.dtype)


def _ffn(xs, layer, wg, wu, wd, cap, t=1024, tf=512, chains=1):
    _, ne, d, fdim = wg.shape
    t, tf = _tile(cap, t), _tile(fdim, tf)
    nt = cap // t
    xs3 = xs.reshape(ne, cap + SUB, d)
    return pl.pallas_call(
        functools.partial(_ffn_kernel, chains=chains),
        grid=(ne, nt, fdim // tf),
        in_specs=[pl.BlockSpec((1, t, d), lambda e, i, f: (e, i, 0)),
                  pl.BlockSpec((None, 1, d, tf), lambda e, i, f: (layer, e, 0, f)),
                  pl.BlockSpec((None, 1, d, tf), lambda e, i, f: (layer, e, 0, f)),
                  pl.BlockSpec((None, 1, tf, d), lambda e, i, f: (layer, e, f, 0))],
        out_specs=pl.BlockSpec((t, d), lambda e, i, f: (e * nt + i, 0)),
        out_shape=jax.ShapeDtypeStruct((ne * cap, d), BF16),
        scratch_shapes=[pltpu.VMEM((t, d), F32)],
        compiler_params=_cparams("parallel", "parallel", "arbitrary"),
        name="expert_ffn",
    )(xs3, wg, wu, wd)


def _combine_kernel(r_ref, x_ref, rank_ref, aff_ref, y_ref, o_ref, ybuf, xbuf, sems, xsem, *, nb, cap):
    b = pl.program_id(0)
    ne, tb = rank_ref.shape
    total = ne * cap
    sub_iota = lax.broadcasted_iota(jnp.int32, (SUB, tb), 0)

    def first_row(bb, e):
        return ((e * cap + r_ref[e * (nb + 1) + bb]) // ROW_TILE) * ROW_TILE

    def fetch(bb, e, slot):
        row = pl.multiple_of(jnp.minimum(first_row(bb, e), total - SUB), ROW_TILE)
        return pltpu.make_async_copy(y_ref.at[pl.ds(row, SUB)], ybuf.at[slot, e // COMBINE_GROUP, pl.ds((e % COMBINE_GROUP) * SUB, SUB)],
                                     sems.at[slot, e])

    @pl.when(b == 0)
    def _():
        for e in range(ne):
            fetch(0, e, 0).start()

    @pl.when(b + 1 < nb)
    def _():
        for e in range(ne):
            fetch(b + 1, e, (b + 1) % 2).start()

    slot = b % 2
    o_ref[...] = x_ref[...]

    def weights(e, start, lower):
        rank_row = rank_ref[pl.ds(e, 1), :]
        row = rank_row + e * cap
        gate = jnp.where(rank_row >= 0, jnp.where(row >= lower, aff_ref[pl.ds(e, 1), :], 0.0), 0.0)
        return jnp.where(sub_iota == row - start, gate, 0.0).astype(BF16)

    def scatter(w, rows):
        return lax.dot_general(w, rows, (((0,), (0,)), ((), ())), preferred_element_type=F32)

    for g in range(ne // COMBINE_GROUP):
        ws = []
        for e in range(g * COMBINE_GROUP, (g + 1) * COMBINE_GROUP):
            fetch(b, e, slot).wait()
            a = first_row(b, e)
            ws.append(weights(e, jnp.minimum(a, total - SUB), a))
        o_ref[...] += scatter(jnp.concatenate(ws, axis=0), ybuf[slot, g])

    def extra(e, _):
        a = first_row(b, e)
        r1 = e * cap + r_ref[e * (nb + 1) + b + 1]

        def sub(s, _):
            lower = a + s * SUB
            start = pl.multiple_of(jnp.minimum(lower, total - SUB), ROW_TILE)
            cp = pltpu.make_async_copy(y_ref.at[pl.ds(start, SUB)], xbuf, xsem)
            cp.start()
            cp.wait()
            o_ref[...] += scatter(weights(e, start, lower), xbuf[...])
            return 0

        lax.fori_loop(1, (r1 - a + SUB - 1) // SUB, sub, 0)
        return 0

    lax.fori_loop(0, ne, extra, 0)


def _combine(r, x, rank, aff_t, y, cap, tb):
    n, d = x.shape
    ne = rank.shape[0]
    nb = n // tb
    return pl.pallas_call(
        functools.partial(_combine_kernel, nb=nb, cap=cap),
        grid_spec=pltpu.PrefetchScalarGridSpec(
            num_scalar_prefetch=1,
            grid=(nb,),
            in_specs=[pl.BlockSpec((tb, d), lambda i, r: (i, 0)),
                      pl.BlockSpec((ne, tb), lambda i, r: (0, i)),
                      pl.BlockSpec((ne, tb), lambda i, r: (0, i)),
                      pl.BlockSpec(memory_space=pl.ANY)],
            out_specs=pl.BlockSpec((tb, d), lambda i, r: (i, 0)),
            scratch_shapes=[pltpu.VMEM((2, ne // COMBINE_GROUP, COMBINE_GROUP * SUB, d), BF16),
                            pltpu.VMEM((SUB, d), BF16),
                            pltpu.SemaphoreType.DMA((2, ne)), pltpu.SemaphoreType.DMA(())],
        ),
        out_shape=jax.ShapeDtypeStruct((n, d), F32),
        compiler_params=_cparams("arbitrary"),
        name="combine",
    )(r, x, rank, aff_t, y)


def _moe(x, h2, aff_t, layer, wg, wu, wd, tb=512):
    n = x.shape[0]
    ne = aff_t.shape[0]
    cap = EC_CAPACITY_FACTOR * n // ne
    tb = _tile(n, tb)
    rank, cum = _select(aff_t, cap)
    r = _block_starts(cum, cap, tb)
    xs = _dispatch(r, h2, rank, cap, tb)
    y = _ffn(xs, layer, wg, wu, wd, cap)
    return _combine(r, x, rank, aff_t, y, cap, tb)


def _rope_tables(seq):
    def tables(dim, reps):
        inv = 1.0 / (ROPE_THETA ** (jnp.arange(0, dim, 2, dtype=F32) / dim))
        ang = jnp.arange(seq, dtype=F32)[:, None] * inv[None, :]
        c, s = jnp.cos(ang), jnp.sin(ang)
        z = jnp.zeros_like(c)
        pad = jnp.zeros((seq, LANES - reps * dim), F32)
        cos = jnp.concatenate([c, c] * reps + [pad], axis=-1)
        s1 = jnp.concatenate([-s, z] * reps + [pad], axis=-1)
        s2 = jnp.concatenate([z, s] * reps + [pad], axis=-1)
        return cos, s1, s2

    return tables(MLA_ROPE, 1), tables(DIFF_QK, LANES // DIFF_QK)


def _prep_layer(l, p):
    d = p["w_in"].shape[1]
    w_in = p["w_in"][l]
    o_cq, o_ckv, o_kpe = 0, Q_LORA, Q_LORA + KV_LORA
    o_qd = o_kpe + MLA_ROPE
    dw = DIFF_HEADS * 2 * DIFF_QK
    o_kd, o_vd = o_qd + dw, o_qd + 2 * dw
    o_g = o_vd + DIFF_HEADS * DIFF_V
    w_in_p = jnp.concatenate(
        [w_in[:, o_g:], w_in[:, o_qd:o_g], w_in[:, :o_qd], jnp.zeros((d, MLA_SEG - o_qd), F32)], axis=1).astype(BF16)
    cols = dict(gate=0, qd=2 * d, kd=2 * d + dw, vd=2 * d + 2 * dw, mla=2 * d + 3 * dw)
    wq = p["w_q_up"][l].reshape(Q_LORA, MLA_HEADS, MLA_QK)
    wq = jnp.pad(wq, ((0, 0), (0, 0), (0, HEAD_PAD - MLA_QK))).reshape(Q_LORA, MLA_HEADS * HEAD_PAD).astype(BF16)
    wkv = p["w_kv_up"][l].reshape(KV_LORA, MLA_HEADS, MLA_NOPE + MLA_V)
    wkv = jnp.concatenate([wkv[:, :, :MLA_NOPE].reshape(KV_LORA, -1), wkv[:, :, MLA_NOPE:].reshape(KV_LORA, -1)],
                          axis=1).astype(BF16)
    row = lambda a: a.reshape(1, -1).astype(F32)
    pad_g = lambda a: jnp.pad(a, (0, HEAD_PAD - MLA_QK)).reshape(1, -1)
    diff_g = lambda a: jnp.stack([jnp.tile(a, LANES // DIFF_QK),
                                  jnp.tile(jnp.roll(a, DIFF_QK // 2), LANES // DIFF_QK)]).astype(F32)
    lam_init = 0.8 - 0.6 * math.exp(-0.3 * l)
    lam = (jnp.exp(jnp.sum(p["lambda_q1"][l].astype(F32) * p["lambda_k1"][l].astype(F32)))
           - jnp.exp(jnp.sum(p["lambda_q2"][l].astype(F32) * p["lambda_k2"][l].astype(F32))) + lam_init)
    return dict(
        cols=cols, w_in=w_in_p, norm_attn_g=row(p["norm_attn_g"][l]), b_gate=row(p["b_gate"][l]),
        cq_g=row(p["mla_cq_g"][l]), ckv_g=row(p["mla_ckv_g"][l]), wq=wq, wkv=wkv,
        q_g=pad_g(p["mla_q_g"][l]), k_g=pad_g(p["mla_k_g"][l]),
        dq_g=diff_g(p["diff_q_g"][l]), dk_g=diff_g(p["diff_k_g"][l]),
        lam=lam.reshape(1).astype(F32), out_scale=1.0 - lam_init, sub_g=row(p["diff_sub_g"][l]),
        wa=p["w_a_out"][l].astype(BF16), wb=p["w_b_out"][l].astype(BF16), wo=p["w_o"][l].astype(BF16),
        norm_ffn_g=row(p["norm_ffn_g"][l]), n_experts=p["w_router"].shape[2],
        wr=jnp.pad(p["w_router"][l].astype(F32), ((0, 0), (0, LANES - p["w_router"].shape[2]))),
        layer=l, wg=p["w_e_gate"], wu=p["w_e_up"], wd=p["w_e_down"],
    )


def _layer(x, lp, ropes, batch, seq):
    m, d = x.shape
    rope_a, rope_b = ropes
    cols = lp["cols"]
    proj = _norm_mm(x, lp["norm_attn_g"], lp["w_in"], BF16)
    q_a, k_a, v_a = _mla_prep(proj, cols["mla"], lp["cq_g"], lp["ckv_g"], lp["wq"], lp["wkv"], lp["q_g"], lp["k_g"],
                              rope_a, seq)
    o_a = _mla_attn(q_a, k_a, v_a, batch, seq)
    q_d, k_d, v_d = _diff_prep(proj, cols["qd"], cols["kd"], cols["vd"], lp["dq_g"], lp["dk_g"], rope_b, seq)
    o_b = _diff_attn(lp["lam"], q_d, k_d, v_d, lp["sub_g"], lp["out_scale"], batch, seq)
    x, h2, aff_t = _merge_out(o_a, o_b, lp["wa"], lp["wb"], proj, cols["gate"], lp["b_gate"], lp["wo"], x,
                              lp["norm_ffn_g"], lp["wr"], lp["n_experts"])
    return _moe(x, h2, aff_t, lp["layer"], lp["wg"], lp["wu"], lp["wd"])


def kernel(x_prompt, x_sample, norm_attn_g, w_in, b_gate, mla_cq_g, w_q_up, mla_ckv_g, w_kv_up, mla_q_g, mla_k_g,
           w_a_out, diff_q_g, diff_k_g, lambda_q1, lambda_k1, lambda_q2, lambda_k2, diff_sub_g, w_b_out, w_o,
           norm_ffn_g, w_router, w_e_gate, w_e_up, w_e_down):
    p = dict(norm_attn_g=norm_attn_g, w_in=w_in, b_gate=b_gate, mla_cq_g=mla_cq_g, w_q_up=w_q_up,
             mla_ckv_g=mla_ckv_g, w_kv_up=w_kv_up, mla_q_g=mla_q_g, mla_k_g=mla_k_g, w_a_out=w_a_out,
             diff_q_g=diff_q_g, diff_k_g=diff_k_g, lambda_q1=lambda_q1, lambda_k1=lambda_k1, lambda_q2=lambda_q2,
             lambda_k2=lambda_k2, diff_sub_g=diff_sub_g, w_b_out=w_b_out, w_o=w_o, norm_ffn_g=norm_ffn_g,
             w_router=w_router, w_e_gate=w_e_gate, w_e_up=w_e_up, w_e_down=w_e_down)
    depth = w_in.shape[0]
    groups = [x_prompt, x_sample]
    shapes = [g.shape for g in groups]
    xs = [g.reshape(-1, g.shape[-1]) for g in groups]
    ropes = [_rope_tables(s[1]) for s in shapes]
    for l in range(depth):
        lp = _prep_layer(l, p)
        xs = [_layer(x, lp, r, s[0], s[1]) for x, r, s in zip(xs, ropes, shapes)]
    return tuple(x.reshape(s) for x, s in zip(xs, shapes))
```

```python
import functools
import math

import jax
import jax.numpy as jnp
from jax import lax
from jax.experimental import pallas as pl
from jax.experimental.pallas import tpu as pltpu

F32 = jnp.float32
BF16 = jnp.bfloat16

MLA_HEADS = 8
MLA_NOPE = 128
MLA_ROPE = 64
MLA_QK = MLA_NOPE + MLA_ROPE
MLA_V = 128
Q_LORA = 768
KV_LORA = 512
DIFF_HEADS = 8
DIFF_QK = 64
DIFF_V = 128
N_EXPERTS = 16
EC_CAPACITY_FACTOR = 2
ROPE_THETA = 10000.0
EPS = 1e-6

LOG2E = 1.4426950408889634
LANES = 128
HEAD_PAD = 2 * LANES
MLA_SEG = 1536
VMEM_LIMIT = 56 * 1024 * 1024
CHAIN_ROWS = 256
MLA_SCORE_ELEMS = 2048 * 4096
DIFF_SCORE_ELEMS = 2048 * 2048


def _cparams(*sem):
    return pltpu.CompilerParams(dimension_semantics=sem, vmem_limit_bytes=VMEM_LIMIT)


def _rms(x, g):
    ms = jnp.mean(x * x, axis=-1, keepdims=True)
    return x * lax.rsqrt(ms + EPS) * g


def _dot(a, b):
    return jnp.dot(a, b, preferred_element_type=F32)


def _dot_nt(a, b, precision=None):
    return lax.dot_general(a, b, (((1,), (1,)), ((), ())), preferred_element_type=F32, precision=precision)


def _sigmoid(z):
    return 1.0 / (1.0 + jnp.exp(-z))


def _tile(n, t):
    t = min(n, t)
    assert n % t == 0, (n, t)
    return t


def _norm_mm_kernel(x_ref, g_ref, w_ref, o_ref, xn_ref):
    @pl.when(pl.program_id(1) == 0)
    def _():
        xn_ref[...] = _rms(x_ref[...].astype(F32), g_ref[...]).astype(BF16)

    o_ref[...] = _dot(xn_ref[...], w_ref[...]).astype(o_ref.dtype)


def _norm_mm(x, g, w, out_dtype, tm=1024, tn=512):
    m, k = x.shape
    n = w.shape[1]
    tm, tn = _tile(m, tm), _tile(n, tn)
    return pl.pallas_call(
        _norm_mm_kernel,
        grid=(m // tm, n // tn),
        in_specs=[
            pl.BlockSpec((tm, k), lambda i, j: (i, 0)),
            pl.BlockSpec((1, k), lambda i, j: (0, 0)),
            pl.BlockSpec((k, tn), lambda i, j: (0, j)),
        ],
        out_specs=pl.BlockSpec((tm, tn), lambda i, j: (i, j)),
        out_shape=jax.ShapeDtypeStruct((m, n), out_dtype),
        scratch_shapes=[pltpu.VMEM((tm, k), BF16)],
        compiler_params=_cparams("parallel", "arbitrary"),
        name="in_proj",
    )(x, g, w)


def _rope128(t, cos, s1, s2):
    return t * cos + pltpu.roll(t, 96, 1) * s1 + pltpu.roll(t, 32, 1) * s2


def _mla_prep_kernel(p0_ref, p1_ref, p2_ref, cqg_ref, ckvg_ref, wq_ref, wkv_ref, qg_ref, kg_ref,
                     cos_ref, s1_ref, s2_ref, q_out, k_out, v_out, *, chains):
    qg, kg = qg_ref[...], kg_ref[...]
    scale = LOG2E / math.sqrt(MLA_QK)
    nk = MLA_HEADS * MLA_NOPE
    rows = p0_ref.shape[0] // chains
    for c in range(chains):
        sl = slice(c * rows, (c + 1) * rows)
        p = jnp.concatenate([p0_ref[sl, :], p1_ref[sl, :], p2_ref[sl, :]], axis=1)
        cq = p[:, :Q_LORA].astype(F32)
        ckv = p[:, Q_LORA:Q_LORA + KV_LORA].astype(F32)
        kpe = p[:, Q_LORA + KV_LORA:Q_LORA + KV_LORA + LANES].astype(F32)
        q = _dot(_rms(cq, cqg_ref[...]).astype(BF16), wq_ref[...])
        kv = _dot(_rms(ckv, ckvg_ref[...]).astype(BF16), wkv_ref[...])
        cos, s1, s2 = cos_ref[sl, :], s1_ref[sl, :], s2_ref[sl, :]
        kpe_sq = jnp.sum(kpe * kpe, axis=-1, keepdims=True)
        kpe_r = _rope128(kpe * kg[:, LANES:], cos, s1, s2)
        for h in range(MLA_HEADS):
            qh = q[:, h * HEAD_PAD:(h + 1) * HEAD_PAD]
            r = lax.rsqrt(jnp.sum(qh * qh, axis=-1, keepdims=True) / MLA_QK + EPS) * scale
            q_out[sl, h * HEAD_PAD:h * HEAD_PAD + LANES] = (qh[:, :LANES] * r * qg[:, :LANES]).astype(BF16)
            q_hi = _rope128(qh[:, LANES:] * qg[:, LANES:], cos, s1, s2) * r
            q_out[sl, h * HEAD_PAD + LANES:(h + 1) * HEAD_PAD] = q_hi.astype(BF16)
            kn = kv[:, h * MLA_NOPE:(h + 1) * MLA_NOPE]
            rk = lax.rsqrt((jnp.sum(kn * kn, axis=-1, keepdims=True) + kpe_sq) / MLA_QK + EPS)
            k_out[sl, h * HEAD_PAD:h * HEAD_PAD + LANES] = (kn * rk * kg[:, :LANES]).astype(BF16)
            k_out[sl, h * HEAD_PAD + LANES:(h + 1) * HEAD_PAD] = (kpe_r * rk).astype(BF16)
            v_out[sl, 2 * h * MLA_V:(2 * h + 1) * MLA_V] = kv[:, nk + h * MLA_V:nk + (h + 1) * MLA_V].astype(BF16)
            v_out[sl, (2 * h + 1) * MLA_V:(2 * h + 2) * MLA_V] = jnp.ones((rows, MLA_V), BF16)


def _mla_prep(proj, mla_col, cqg, ckvg, wq, wkv, qg, kg, rope, seq, tm=1024):
    m = proj.shape[0]
    tm = _tile(seq, tm)
    chains = max(1, tm // CHAIN_ROWS)
    nsb = seq // tm
    pw = MLA_SEG // 3
    pspec = lambda k: pl.BlockSpec((tm, pw), lambda i: (i, mla_col // pw + k))
    full = lambda a: pl.BlockSpec(a.shape, lambda i: (0,) * a.ndim)
    rspec = pl.BlockSpec((tm, LANES), lambda i: (i % nsb, 0))
    hq = MLA_HEADS * HEAD_PAD
    return pl.pallas_call(
        functools.partial(_mla_prep_kernel, chains=chains),
        grid=(m // tm,),
        in_specs=[pspec(0), pspec(1), pspec(2), full(cqg), full(ckvg), full(wq), full(wkv),
                  full(qg), full(kg), rspec, rspec, rspec],
        out_specs=[pl.BlockSpec((tm, hq), lambda i: (i, 0)), pl.BlockSpec((tm, hq), lambda i: (i, 0)),
                   pl.BlockSpec((tm, 2 * MLA_HEADS * MLA_V), lambda i: (i, 0))],
        out_shape=[jax.ShapeDtypeStruct((m, hq), BF16), jax.ShapeDtypeStruct((m, hq), BF16),
                   jax.ShapeDtypeStruct((m, 2 * MLA_HEADS * MLA_V), BF16)],
        compiler_params=_cparams("parallel"),
        name="mla_prep",
    )(proj, proj, proj, cqg, ckvg, wq, wkv, qg, kg, *rope)


def _diff_prep_kernel(q_ref, k_ref, v_ref, qg_ref, kg_ref, cos_ref, s1_ref, s2_ref, q_out, k_out, v_out):
    cos, sin = cos_ref[...], s2_ref[...] - s1_ref[...]
    nh = q_ref.shape[1] // LANES
    i = lax.broadcasted_iota(jnp.int32, (LANES, LANES), 0)
    j = lax.broadcasted_iota(jnp.int32, (LANES, LANES), 1)
    half = DIFF_QK // 2
    first = (j % DIFF_QK) < half
    same = jnp.where(i // DIFF_QK == j // DIFF_QK, 1.0, 0.0).astype(BF16)
    perm = jnp.where(i == j + half, jnp.where(first, -1.0, 0.0),
                     jnp.where(i == j - half, jnp.where(first, 0.0, 1.0), 0.0)).astype(BF16)

    def prep(ref, g_ref, out, scale):
        t1 = g_ref[0:1, :] * cos
        t2 = g_ref[1:2, :] * sin
        for h in range(nh):
            xb = ref[:, h * LANES:(h + 1) * LANES]
            x = xb.astype(F32)
            ssq = _dot((x * x).astype(BF16), same)
            r = lax.rsqrt(ssq / DIFF_QK + EPS) * scale
            out[:, h * LANES:(h + 1) * LANES] = ((x * t1 + _dot(xb, perm) * t2) * r).astype(BF16)

    prep(q_ref, qg_ref, q_out, LOG2E / math.sqrt(DIFF_QK))
    prep(k_ref, kg_ref, k_out, 1.0)
    for h in range(nh):
        v_out[:, 2 * h * LANES:(2 * h + 1) * LANES] = v_ref[:, h * LANES:(h + 1) * LANES]
        v_out[:, (2 * h + 1) * LANES:(2 * h + 2) * LANES] = jnp.ones((v_ref.shape[0], LANES), BF16)


def _diff_prep(proj, q_col, k_col, v_col, qg, kg, rope, seq, tm=512, tn=512):
    m = proj.shape[0]
    width = DIFF_HEADS * 2 * DIFF_QK
    tm = _tile(seq, tm)
    nsb = seq // tm
    nj = width // tn
    full = lambda a: pl.BlockSpec(a.shape, lambda i, j: (0,) * a.ndim)
    rspec = pl.BlockSpec((tm, LANES), lambda i, j: (i % nsb, 0))
    ospec = pl.BlockSpec((tm, tn), lambda i, j: (i, j))
    return pl.pallas_call(
        _diff_prep_kernel,
        grid=(m // tm, nj),
        in_specs=[pl.BlockSpec((tm, tn), lambda i, j: (i, q_col // tn + j)),
                  pl.BlockSpec((tm, tn), lambda i, j: (i, k_col // tn + j)),
                  pl.BlockSpec((tm, tn), lambda i, j: (i, v_col // tn + j)),
                  full(qg), full(kg), rspec, rspec, rspec],
        out_specs=[ospec, ospec, pl.BlockSpec((tm, 2 * tn), lambda i, j: (i, j))],
        out_shape=[jax.ShapeDtypeStruct((m, width), BF16)] * 2 + [jax.ShapeDtypeStruct((m, 2 * width), BF16)],
        compiler_params=_cparams("parallel", "parallel"),
        name="diff_prep",
    )(proj, proj, proj, qg, kg, *rope)


def _softmax_pv(q, k, v1):
    s = _dot_nt(q, k)
    p = jnp.exp2((s - jnp.max(s, axis=-1, keepdims=True)).astype(BF16))
    o = _dot(p, v1)
    return o[:, :LANES] / o[:, LANES:]


def _mla_attn_kernel(q_ref, k_ref, v_ref, o_ref, *, chains):
    rows = q_ref.shape[0] // chains
    k, v1 = k_ref[...], v_ref[...]
    for c in range(chains):
        sl = slice(c * rows, (c + 1) * rows)
        o_ref[sl, :] = _softmax_pv(q_ref[sl, :], k, v1).astype(o_ref.dtype)


def _attn_rows(seq, score_elems):
    return max(CHAIN_ROWS, min(seq, score_elems // seq))


def _mla_attn(q, k, v1, batch, seq):
    m = q.shape[0]
    tq = _attn_rows(seq, MLA_SCORE_ELEMS)
    chains = tq // CHAIN_ROWS
    nq = seq // tq
    return pl.pallas_call(
        functools.partial(_mla_attn_kernel, chains=chains),
        grid=(batch, MLA_HEADS, nq),
        in_specs=[pl.BlockSpec((tq, HEAD_PAD), lambda b, h, i: (b * nq + i, h)),
                  pl.BlockSpec((seq, HEAD_PAD), lambda b, h, i: (b, h)),
                  pl.BlockSpec((seq, 2 * MLA_V), lambda b, h, i: (b, h))],
        out_specs=pl.BlockSpec((tq, MLA_V), lambda b, h, i: (b * nq + i, h)),
        out_shape=jax.ShapeDtypeStruct((m, MLA_HEADS * MLA_V), BF16),
        compiler_params=_cparams("parallel", "parallel", "arbitrary"),
        name="mla_attn",
    )(q, k, v1)


def _diff_attn_kernel(lam_ref, q_ref, k_ref, v_ref, g_ref, o_ref, *, out_scale, chains):
    rows = q_ref.shape[0] // chains
    k, v1 = k_ref[...], v_ref[...]
    lo = lax.broadcasted_iota(jnp.int32, (1, LANES), 1) < DIFF_QK
    for c in range(chains):
        sl = slice(c * rows, (c + 1) * rows)
        q = q_ref[sl, :]
        zero = jnp.zeros_like(q)
        o1 = _softmax_pv(jnp.where(lo, q, zero), k, v1)
        o2 = _softmax_pv(jnp.where(lo, zero, q), k, v1)
        o = o1 - lam_ref[0] * o2
        o_ref[sl, :] = (_rms(o, g_ref[...]) * out_scale).astype(o_ref.dtype)


def _diff_attn(lam, q, k, v1, sub_g, out_scale, batch, seq):
    m = q.shape[0]
    tq = _attn_rows(seq, DIFF_SCORE_ELEMS)
    chains = tq // CHAIN_ROWS
    nq = seq // tq
    return pl.pallas_call(
        functools.partial(_diff_attn_kernel, out_scale=out_scale, chains=chains),
        grid=(batch, DIFF_HEADS, nq),
        in_specs=[pl.BlockSpec(memory_space=pltpu.SMEM),
                  pl.BlockSpec((tq, LANES), lambda b, h, i: (b * nq + i, h)),
                  pl.BlockSpec((seq, LANES), lambda b, h, i: (b, h)),
                  pl.BlockSpec((seq, 2 * DIFF_V), lambda b, h, i: (b, h)),
                  pl.BlockSpec((1, DIFF_V), lambda b, h, i: (0, 0))],
        out_specs=pl.BlockSpec((tq, DIFF_V), lambda b, h, i: (b * nq + i, h)),
        out_shape=jax.ShapeDtypeStruct((m, DIFF_HEADS * DIFF_V), BF16),
        compiler_params=_cparams("parallel", "parallel", "arbitrary"),
        name="diff_attn",
    )(lam, q, k, v1, sub_g)


def _merge_out_kernel(oa_ref, ob_ref, wa_ref, wb_ref, g0_ref, g1_ref, b_ref, wo_ref, x_ref, ng_ref, wr_ref,
                      o_ref, h_ref, aff_ref):
    d = wa_ref.shape[1]
    a = _dot(oa_ref[...], wa_ref[...])
    b = _dot(ob_ref[...], wb_ref[...])
    g0 = _sigmoid(g0_ref[...].astype(F32) + b_ref[:, :d])
    g1 = _sigmoid(g1_ref[...].astype(F32) + b_ref[:, d:])
    merged = (g0 * a + g1 * b).astype(BF16)
    x = x_ref[...] + _dot(merged, wo_ref[...])
    o_ref[...] = x
    xn = _rms(x, ng_ref[...])
    h_ref[...] = xn.astype(BF16)
    wr = wr_ref[...]
    xh, wh = xn.astype(BF16), wr.astype(BF16)
    xl, wl = (xn - xh.astype(F32)).astype(BF16), (wr - wh.astype(F32)).astype(BF16)
    lg = _dot(xh, wh) + (_dot(xh, wl) + _dot(xl, wh))
    lg = lg.T[:aff_ref.shape[0]]
    e = jnp.exp(lg - jnp.max(lg, axis=0, keepdims=True))
    aff_ref[...] = e / jnp.sum(e, axis=0, keepdims=True)


def _merge_out(oa, ob, wa, wb, proj, gate_col, b_gate, wo, x, norm_g, wr, ne, tm=256):
    m, d = x.shape
    tm = _tile(m, tm)
    g0b = gate_col // d
    resident = lambda a: pl.BlockSpec(a.shape, lambda i: (0,) * a.ndim, pipeline_mode=pl.Buffered(1))
    row = lambda a: pl.BlockSpec((tm, a.shape[1]), lambda i: (i, 0))
    return pl.pallas_call(
        _merge_out_kernel,
        grid=(m // tm,),
        in_specs=[row(oa), row(ob), resident(wa), resident(wb),
                  pl.BlockSpec((tm, d), lambda i: (i, g0b)),
                  pl.BlockSpec((tm, d), lambda i: (i, g0b + 1)),
                  resident(b_gate), resident(wo), row(x), resident(norm_g), resident(wr)],
        out_specs=[pl.BlockSpec((tm, d), lambda i: (i, 0)), pl.BlockSpec((tm, d), lambda i: (i, 0)),
                   pl.BlockSpec((ne, tm), lambda i: (0, i))],
        out_shape=[jax.ShapeDtypeStruct((m, d), F32), jax.ShapeDtypeStruct((m, d), BF16),
                   jax.ShapeDtypeStruct((ne, m), F32)],
        compiler_params=_cparams("parallel"),
        name="merge_out",
    )(oa, ob, wa, wb, proj, proj, b_gate, wo, x, norm_g, wr)


def _select_kernel(aff_ref, rank_ref, cum_ref, *, cap):
    ne, n = aff_ref.shape
    nch = n // LANES

    def count(mask):
        return jnp.sum(jnp.where(mask, 1.0, 0.0), axis=1, keepdims=True)

    def bisect(i, thr):
        cand = thr | jnp.left_shift(jnp.int32(1), 30 - i)
        bits = pltpu.bitcast(aff_ref[...], jnp.int32)
        return jnp.where(count(bits >= cand) >= cap, cand, thr)

    thr = lax.fori_loop(0, 31, bisect, jnp.zeros((ne, 1), jnp.int32))
    need = cap - count(pltpu.bitcast(aff_ref[...], jnp.int32) > thr)
    tri = jnp.where(lax.broadcasted_iota(jnp.int32, (LANES, LANES), 0)
                    <= lax.broadcasted_iota(jnp.int32, (LANES, LANES), 1), 1.0, 0.0).astype(BF16)

    def chunk(j, carry):
        c_eq, c_sel = carry
        off = pl.multiple_of(j * LANES, LANES)
        bits = pltpu.bitcast(aff_ref[:, pl.ds(off, LANES)], jnp.int32)
        eq = jnp.where(bits == thr, 1.0, 0.0)
        eq_incl = _dot(eq.astype(BF16), tri)
        take = jnp.where(c_eq + eq_incl - eq < need, eq, 0.0)
        sel = jnp.where(bits > thr, 1.0, take)
        sel_incl = _dot(sel.astype(BF16), tri)
        rank = c_sel + sel_incl - sel
        rank_ref[:, pl.ds(off, LANES)] = jnp.where(sel > 0.0, rank, -1.0).astype(jnp.int32)
        cum_ref[j] = jnp.broadcast_to(c_sel, (ne, LANES)).astype(jnp.int32)
        return c_eq + eq_incl[:, LANES - 1:], c_sel + sel_incl[:, LANES - 1:]

    zero = jnp.zeros((ne, 1), F32)
    lax.fori_loop(0, nch, chunk, (zero, zero))


def _select(aff_t, cap):
    ne, n = aff_t.shape
    nch = n // LANES
    return pl.pallas_call(
        functools.partial(_select_kernel, cap=cap),
        grid=(1,),
        in_specs=[pl.BlockSpec((ne, n), lambda i: (0, 0))],
        out_specs=[pl.BlockSpec((ne, n), lambda i: (0, 0)), pl.BlockSpec((nch, ne, LANES), lambda i: (0, 0, 0))],
        out_shape=[jax.ShapeDtypeStruct((ne, n), jnp.int32), jax.ShapeDtypeStruct((nch, ne, LANES), jnp.int32)],
        compiler_params=_cparams("arbitrary"),
        name="select",
    )(aff_t)


SUB = 128
ROW_TILE = 16
DISPATCH_GROUP = 4
COMBINE_GROUP = 8


def _block_starts(cum, cap, tb):
    r = cum[::tb // LANES, :, 0]
    r = jnp.concatenate([r, jnp.full((1, r.shape[1]), cap, jnp.int32)], axis=0)
    return r.T.reshape(-1)


def _dispatch_kernel(r_ref, h_ref, rank_ref, xs_ref, stage, xstage, carry, sems, xsem, *, nb, cap_pad):
    b = pl.program_id(0)
    ne, tb = rank_ref.shape
    sub_iota = lax.broadcasted_iota(jnp.int32, (SUB, tb), 0)

    def copy(e, row):
        return pltpu.make_async_copy(stage.at[e], xs_ref.at[pl.ds(row, SUB)], sems.at[e])

    def one_hot(e, base):
        return jnp.where(sub_iota == rank_ref[pl.ds(e, 1), :] - base, 1.0, 0.0).astype(BF16)

    def run(e):
        r1 = r_ref[e * (nb + 1) + b + 1]
        a = (r_ref[e * (nb + 1) + b] // ROW_TILE) * ROW_TILE
        return a, (r1 // ROW_TILE) * ROW_TILE, (r1 - a) // SUB + 1

    @pl.when(b == 0)
    def _():
        carry[...] = jnp.zeros_like(carry)
        for e in range(ne):
            stage[e] = jnp.zeros(stage.shape[1:], BF16)
            pad = copy(e, (e + 1) * cap_pad - SUB)
            pad.start()
            pad.wait()

    @pl.when(b > 0)
    def _():
        for e in range(ne):
            copy(e, 0).wait()

    for g in range(ne // DISPATCH_GROUP):
        experts = range(g * DISPATCH_GROUP, (g + 1) * DISPATCH_GROUP)
        runs = [run(e) for e in experts]
        q = jnp.concatenate([one_hot(e, a) for e, (a, _, _) in zip(experts, runs)], axis=0)
        rows = _dot(q, h_ref[...])
        for k, (e, (a, a_next, nsub)) in enumerate(zip(experts, runs)):
            part = rows[k * SUB:(k + 1) * SUB]
            stage[e] = part.astype(BF16)
            stage[e, pl.ds(0, ROW_TILE), :] = (part[:ROW_TILE] + carry[e]).astype(BF16)

            @pl.when(nsub == 1)
            def _():
                carry[e] = stage[e, pl.ds(pl.multiple_of(a_next - a, ROW_TILE), ROW_TILE), :].astype(F32)

            copy(e, pl.multiple_of(e * cap_pad + a, ROW_TILE)).start()

    def longer(e, _):
        a, a_next, nsub = run(e)

        def sub(s, _):
            base = a + s * SUB
            xstage[...] = _dot(one_hot(e, base), h_ref[...]).astype(BF16)

            @pl.when(s == nsub - 1)
            def _():
                carry[e] = xstage[pl.ds(pl.multiple_of(a_next - base, ROW_TILE), ROW_TILE), :].astype(F32)

            cp = pltpu.make_async_copy(xstage, xs_ref.at[pl.ds(pl.multiple_of(e * cap_pad + base, ROW_TILE), SUB)], xsem)
            cp.start()
            cp.wait()
            return 0

        lax.fori_loop(1, nsub, sub, 0)
        return 0

    lax.fori_loop(0, ne, longer, 0)

    @pl.when(b == nb - 1)
    def _():
        for e in range(ne):
            copy(e, 0).wait()


def _dispatch(r, h2, rank, cap, tb):
    n, d = h2.shape
    ne = rank.shape[0]
    nb = n // tb
    cap_pad = cap + SUB
    return pl.pallas_call(
        functools.partial(_dispatch_kernel, nb=nb, cap_pad=cap_pad),
        grid_spec=pltpu.PrefetchScalarGridSpec(
            num_scalar_prefetch=1,
            grid=(nb,),
            in_specs=[pl.BlockSpec((tb, d), lambda i, r: (i, 0)), pl.BlockSpec((ne, tb), lambda i, r: (0, i))],
            out_specs=pl.BlockSpec(memory_space=pl.ANY),
            scratch_shapes=[pltpu.VMEM((ne, SUB, d), BF16), pltpu.VMEM((SUB, d), BF16),
                            pltpu.VMEM((ne, ROW_TILE, d), F32),
                            pltpu.SemaphoreType.DMA((ne,)), pltpu.SemaphoreType.DMA(())],
        ),
        out_shape=jax.ShapeDtypeStruct((ne * cap_pad, d), BF16),
        compiler_params=_cparams("arbitrary"),
        name="dispatch",
    )(r, h2, rank)


def _ffn_kernel(x_ref, wg_ref, wu_ref, wd_ref, o_ref, acc_ref, *, chains):
    f = pl.program_id(2)
    rows = x_ref.shape[1] // chains

    @pl.when(f == 0)
    def _():
        acc_ref[...] = jnp.zeros_like(acc_ref)

    wg, wu, wd = wg_ref[0].astype(BF16), wu_ref[0].astype(BF16), wd_ref[0].astype(BF16)
    for c in range(chains):
        sl = slice(c * rows, (c + 1) * rows)
        x = x_ref[0, sl, :]
        hg = _dot(x, wg)
        hu = _dot(x, wu)
        hid = (hg * _sigmoid(hg) * hu).astype(BF16)
        acc_ref[sl, :] += _dot(hid, wd)

    @pl.when(f == pl.num_programs(2) - 1)
    def _():
        o_ref[...] = acc_ref[...].astype(o_ref.dtype)


def _ffn(xs, layer, wg, wu, wd, cap, t=1024, tf=512, chains=1):
    _, ne, d, fdim = wg.shape
    t, tf = _tile(cap, t), _tile(fdim, tf)
    nt = cap // t
    xs3 = xs.reshape(ne, cap + SUB, d)
    return pl.pallas_call(
        functools.partial(_ffn_kernel, chains=chains),
        grid=(ne, nt, fdim // tf),
        in_specs=[pl.BlockSpec((1, t, d), lambda e, i, f: (e, i, 0)),
                  pl.BlockSpec((None, 1, d, tf), lambda e, i, f: (layer, e, 0, f)),
                  pl.BlockSpec((None, 1, d, tf), lambda e, i, f: (layer, e, 0, f)),
                  pl.BlockSpec((None, 1, tf, d), lambda e, i, f: (layer, e, f, 0))],
        out_specs=pl.BlockSpec((t, d), lambda e, i, f: (e * nt + i, 0)),
        out_shape=jax.ShapeDtypeStruct((ne * cap, d), BF16),
        scratch_shapes=[pltpu.VMEM((t, d), F32)],
        compiler_params=_cparams("parallel", "parallel", "arbitrary"),
        name="expert_ffn",
    )(xs3, wg, wu, wd)


def _combine_kernel(r_ref, x_ref, rank_ref, aff_ref, y_ref, o_ref, ybuf, xbuf, sems, xsem, *, nb, cap):
    b = pl.program_id(0)
    ne, tb = rank_ref.shape
    total = ne * cap
    sub_iota = lax.broadcasted_iota(jnp.int32, (SUB, tb), 0)

    def first_row(bb, e):
        return ((e * cap + r_ref[e * (nb + 1) + bb]) // ROW_TILE) * ROW_TILE

    def fetch(bb, e, slot):
        row = pl.multiple_of(jnp.minimum(first_row(bb, e), total - SUB), ROW_TILE)
        return pltpu.make_async_copy(y_ref.at[pl.ds(row, SUB)], ybuf.at[slot, e // COMBINE_GROUP, pl.ds((e % COMBINE_GROUP) * SUB, SUB)],
                                     sems.at[slot, e])

    @pl.when(b == 0)
    def _():
        for e in range(ne):
            fetch(0, e, 0).start()

    @pl.when(b + 1 < nb)
    def _():
        for e in range(ne):
            fetch(b + 1, e, (b + 1) % 2).start()

    slot = b % 2
    o_ref[...] = x_ref[...]

    def weights(e, start, lower):
        rank_row = rank_ref[pl.ds(e, 1), :]
        row = rank_row + e * cap
        gate = jnp.where(rank_row >= 0, jnp.where(row >= lower, aff_ref[pl.ds(e, 1), :], 0.0), 0.0)
        return jnp.where(sub_iota == row - start, gate, 0.0).astype(BF16)

    def scatter(w, rows):
        return lax.dot_general(w, rows, (((0,), (0,)), ((), ())), preferred_element_type=F32)

    for g in range(ne // COMBINE_GROUP):
        ws = []
        for e in range(g * COMBINE_GROUP, (g + 1) * COMBINE_GROUP):
            fetch(b, e, slot).wait()
            a = first_row(b, e)
            ws.append(weights(e, jnp.minimum(a, total - SUB), a))
        o_ref[...] += scatter(jnp.concatenate(ws, axis=0), ybuf[slot, g])

    def extra(e, _):
        a = first_row(b, e)
        r1 = e * cap + r_ref[e * (nb + 1) + b + 1]

        def sub(s, _):
            lower = a + s * SUB
            start = pl.multiple_of(jnp.minimum(lower, total - SUB), ROW_TILE)
            cp = pltpu.make_async_copy(y_ref.at[pl.ds(start, SUB)], xbuf, xsem)
            cp.start()
            cp.wait()
            o_ref[...] += scatter(weights(e, start, lower), xbuf[...])
            return 0

        lax.fori_loop(1, (r1 - a + SUB - 1) // SUB, sub, 0)
        return 0

    lax.fori_loop(0, ne, extra, 0)


def _combine(r, x, rank, aff_t, y, cap, tb):
    n, d = x.shape
    ne = rank.shape[0]
    nb = n // tb
    return pl.pallas_call(
        functools.partial(_combine_kernel, nb=nb, cap=cap),
        grid_spec=pltpu.PrefetchScalarGridSpec(
            num_scalar_prefetch=1,
            grid=(nb,),
            in_specs=[pl.BlockSpec((tb, d), lambda i, r: (i, 0)),
                      pl.BlockSpec((ne, tb), lambda i, r: (0, i)),
                      pl.BlockSpec((ne, tb), lambda i, r: (0, i)),
                      pl.BlockSpec(memory_space=pl.ANY)],
            out_specs=pl.BlockSpec((tb, d), lambda i, r: (i, 0)),
            scratch_shapes=[pltpu.VMEM((2, ne // COMBINE_GROUP, COMBINE_GROUP * SUB, d), BF16),
                            pltpu.VMEM((SUB, d), BF16),
                            pltpu.SemaphoreType.DMA((2, ne)), pltpu.SemaphoreType.DMA(())],
        ),
        out_shape=jax.ShapeDtypeStruct((n, d), F32),
        compiler_params=_cparams("arbitrary"),
        name="combine",
    )(r, x, rank, aff_t, y)


def _moe(x, h2, aff_t, layer, wg, wu, wd, tb=512):
    n = x.shape[0]
    ne = aff_t.shape[0]
    cap = EC_CAPACITY_FACTOR * n // ne
    tb = _tile(n, tb)
    rank, cum = _select(aff_t, cap)
    r = _block_starts(cum, cap, tb)
    xs = _dispatch(r, h2, rank, cap, tb)
    y = _ffn(xs, layer, wg, wu, wd, cap)
    return _combine(r, x, rank, aff_t, y, cap, tb)


def _rope_tables(seq):
    def tables(dim, reps):
        inv = 1.0 / (ROPE_THETA ** (jnp.arange(0, dim, 2, dtype=F32) / dim))
        ang = jnp.arange(seq, dtype=F32)[:, None] * inv[None, :]
        c, s = jnp.cos(ang), jnp.sin(ang)
        z = jnp.zeros_like(c)
        pad = jnp.zeros((seq, LANES - reps * dim), F32)
        cos = jnp.concatenate([c, c] * reps + [pad], axis=-1)
        s1 = jnp.concatenate([-s, z] * reps + [pad], axis=-1)
        s2 = jnp.concatenate([z, s] * reps + [pad], axis=-1)
        return cos, s1, s2

    return tables(MLA_ROPE, 1), tables(DIFF_QK, LANES // DIFF_QK)


def _prep_layer(l, p):
    d = p["w_in"].shape[1]
    w_in = p["w_in"][l]
    o_cq, o_ckv, o_kpe = 0, Q_LORA, Q_LORA + KV_LORA
    o_qd = o_kpe + MLA_ROPE
    dw = DIFF_HEADS * 2 * DIFF_QK
    o_kd, o_vd = o_qd + dw, o_qd + 2 * dw
    o_g = o_vd + DIFF_HEADS * DIFF_V
    w_in_p = jnp.concatenate(
        [w_in[:, o_g:], w_in[:, o_qd:o_g], w_in[:, :o_qd], jnp.zeros((d, MLA_SEG - o_qd), F32)], axis=1).astype(BF16)
    cols = dict(gate=0, qd=2 * d, kd=2 * d + dw, vd=2 * d + 2 * dw, mla=2 * d + 3 * dw)
    wq = p["w_q_up"][l].reshape(Q_LORA, MLA_HEADS, MLA_QK)
    wq = jnp.pad(wq, ((0, 0), (0, 0), (0, HEAD_PAD - MLA_QK))).reshape(Q_LORA, MLA_HEADS * HEAD_PAD).astype(BF16)
    wkv = p["w_kv_up"][l].reshape(KV_LORA, MLA_HEADS, MLA_NOPE + MLA_V)
    wkv = jnp.concatenate([wkv[:, :, :MLA_NOPE].reshape(KV_LORA, -1), wkv[:, :, MLA_NOPE:].reshape(KV_LORA, -1)],
                          axis=1).astype(BF16)
    row = lambda a: a.reshape(1, -1).astype(F32)
    pad_g = lambda a: jnp.pad(a, (0, HEAD_PAD - MLA_QK)).reshape(1, -1)
    diff_g = lambda a: jnp.stack([jnp.tile(a, LANES // DIFF_QK),
                                  jnp.tile(jnp.roll(a, DIFF_QK // 2), LANES // DIFF_QK)]).astype(F32)
    lam_init = 0.8 - 0.6 * math.exp(-0.3 * l)
    lam = (jnp.exp(jnp.sum(p["lambda_q1"][l].astype(F32) * p["lambda_k1"][l].astype(F32)))
           - jnp.exp(jnp.sum(p["lambda_q2"][l].astype(F32) * p["lambda_k2"][l].astype(F32))) + lam_init)
    return dict(
        cols=cols, w_in=w_in_p, norm_attn_g=row(p["norm_attn_g"][l]), b_gate=row(p["b_gate"][l]),
        cq_g=row(p["mla_cq_g"][l]), ckv_g=row(p["mla_ckv_g"][l]), wq=wq, wkv=wkv,
        q_g=pad_g(p["mla_q_g"][l]), k_g=pad_g(p["mla_k_g"][l]),
        dq_g=diff_g(p["diff_q_g"][l]), dk_g=diff_g(p["diff_k_g"][l]),
        lam=lam.reshape(1).astype(F32), out_scale=1.0 - lam_init, sub_g=row(p["diff_sub_g"][l]),
        wa=p["w_a_out"][l].astype(BF16), wb=p["w_b_out"][l].astype(BF16), wo=p["w_o"][l].astype(BF16),
        norm_ffn_g=row(p["norm_ffn_g"][l]), n_experts=p["w_router"].shape[2],
        wr=jnp.pad(p["w_router"][l].astype(F32), ((0, 0), (0, LANES - p["w_router"].shape[2]))),
        layer=l, wg=p["w_e_gate"], wu=p["w_e_up"], wd=p["w_e_down"],
    )


def _layer(x, lp, ropes, batch, seq):
    m, d = x.shape
    rope_a, rope_b = ropes
    cols = lp["cols"]
    proj = _norm_mm(x, lp["norm_attn_g"], lp["w_in"], BF16)
    q_a, k_a, v_a = _mla_prep(proj, cols["mla"], lp["cq_g"], lp["ckv_g"], lp["wq"], lp["wkv"], lp["q_g"], lp["k_g"],
                              rope_a, seq)
    o_a = _mla_attn(q_a, k_a, v_a, batch, seq)
    q_d, k_d, v_d = _diff_prep(proj, cols["qd"], cols["kd"], cols["vd"], lp["dq_g"], lp["dk_g"], rope_b, seq)
    o_b = _diff_attn(lp["lam"], q_d, k_d, v_d, lp["sub_g"], lp["out_scale"], batch, seq)
    x, h2, aff_t = _merge_out(o_a, o_b, lp["wa"], lp["wb"], proj, cols["gate"], lp["b_gate"], lp["wo"], x,
                              lp["norm_ffn_g"], lp["wr"], lp["n_experts"])
    return _moe(x, h2, aff_t, lp["layer"], lp["wg"], lp["wu"], lp["wd"])


def kernel(x_prompt, x_sample, norm_attn_g, w_in, b_gate, mla_cq_g, w_q_up, mla_ckv_g, w_kv_up, mla_q_g, mla_k_g,
           w_a_out, diff_q_g, diff_k_g, lambda_q1, lambda_k1, lambda_q2, lambda_k2, diff_sub_g, w_b_out, w_o,
           norm_ffn_g, w_router, w_e_gate, w_e_up, w_e_down):
    p = dict(norm_attn_g=norm_attn_g, w_in=w_in, b_gate=b_gate, mla_cq_g=mla_cq_g, w_q_up=w_q_up,
             mla_ckv_g=mla_ckv_g, w_kv_up=w_kv_up, mla_q_g=mla_q_g, mla_k_g=mla_k_g, w_a_out=w_a_out,
             diff_q_g=diff_q_g, diff_k_g=diff_k_g, lambda_q1=lambda_q1, lambda_k1=lambda_k1, lambda_q2=lambda_q2,
             lambda_k2=lambda_k2, diff_sub_g=diff_sub_g, w_b_out=w_b_out, w_o=w_o, norm_ffn_g=norm_ffn_g,
             w_router=w_router, w_e_gate=w_e_gate, w_e_up=w_e_up, w_e_down=w_e_down)
    depth = w_in.shape[0]
    groups = [x_prompt, x_sample]
    shapes = [g.shape for g in groups]
    xs = [g.reshape(-1, g.shape[-1]) for g in groups]
    ropes = [_rope_tables(s[1]) for s in shapes]
    for l in range(depth):
        lp = _prep_layer(l, p)
        xs = [_layer(x, lp, r, s[0], s[1]) for x, r, s in zip(xs, ropes, shapes)]
    return tuple(x.reshape(s) for x, s in zip(xs, shapes))
```
